```python
import math
import jax, jax.numpy as jnp
from jax import lax
import numpy as np

D_MODEL = 1024
BATCH = 32
SEQ = 2048
DEPTH = 2
DEC_BATCH = 4
DEC_SEQ = 4096
PAST_LEN = 128

HEAD_DIM = 64
A_GROUPS = ((128, 1), (512, 4), (2048, 16))
A_N_GROUPS = 3
A_HEADS = 4
A_WIDTH = A_HEADS * HEAD_DIM
A_COLS = A_N_GROUPS * 3 * A_WIDTH
B_Q_HEADS = 8
B_KV_HEADS = 2
B_GROUP = B_Q_HEADS // B_KV_HEADS
B_Q_WIDTH = B_Q_HEADS * HEAD_DIM
B_KV_WIDTH = B_KV_HEADS * HEAD_DIM
B_HALF_WINDOW = 128
N_IN = A_COLS + B_Q_WIDTH + 2 * B_KV_WIDTH
D_FF = 2816
N_EXPERTS = 8
TOP_K = 2
D_FF_EXPERT = 3584
MOE_BLOCK = 128
N_DENSE = (DEPTH + 1) // 2
N_MOE = DEPTH // 2
RMS_EPS = 1e-6
NEG_INF = -1e30

kernel_name = "hybrid_dilated_swa_sink_moe_encoder"


def rms_norm(x, g):
    xf = x.astype(jnp.float32)
    y = xf * lax.rsqrt(jnp.mean(xf * xf, axis=-1, keepdims=True) + RMS_EPS) * g.astype(jnp.float32)
    return y.astype(x.dtype)


def alibi_slopes(n):
    return jnp.asarray(2.0 ** (-8.0 * np.arange(1, n + 1) / n), dtype=jnp.float32)


def banded_attention(q, k, v, half_window, step, slopes, sink=None):
    bsz, n, hkv, g, hd = q.shape
    blk = half_window
    nb = -(-n // blk)
    npad = nb * blk
    qp = jnp.pad(q, ((0, 0), (0, npad - n), (0, 0), (0, 0), (0, 0))).reshape(bsz, nb, blk, hkv, g, hd)
    kv_pad = ((0, 0), (blk, npad - n + blk), (0, 0), (0, 0))
    kp = jnp.pad(k, kv_pad)
    vp = jnp.pad(v, kv_pad)

    def windows(a):
        return jnp.concatenate(
            [a[:, o * blk:o * blk + npad].reshape(bsz, nb, blk, hkv, hd) for o in range(3)], axis=2)

    kw = windows(kp)
    vw = windows(vp).astype(jnp.float32)
    s = jnp.einsum('bnqhgd,bnkhd->bnhgqk', qp, kw, preferred_element_type=jnp.float32) * (hd ** -0.5)
    blocks = jnp.arange(nb)[:, None]
    qpos = blocks * blk + jnp.arange(blk)[None, :]
    kpos = blocks * blk - blk + jnp.arange(3 * blk)[None, :]
    rel = kpos[:, None, :] - qpos[:, :, None]
    valid = (jnp.abs(rel) <= half_window) & (kpos[:, None, :] >= 0) & (kpos[:, None, :] < n)
    dist = jnp.abs(rel).astype(jnp.float32) * step
    bias = -slopes.astype(jnp.float32)[None, :, :, None, None] * dist[:, None, None]
    s = jnp.where(valid[:, None, None], s + bias, NEG_INF)
    m = jnp.max(s, axis=-1)
    if sink is not None:
        sk = sink.astype(jnp.float32)[None, None, :, :, None]
        m = jnp.maximum(m, sk)
    p = jnp.exp(s - m[..., None])
    l = jnp.sum(p, axis=-1)
    if sink is not None:
        l = l + jnp.exp(sk - m)
    o = jnp.einsum('bnhgqk,bnkhd->bnqhgd', p, vw)
    out = (o / jnp.moveaxis(l, -1, 2)[..., None]).reshape(bsz, npad, hkv, g, hd)[:, :n]
    lse = jnp.moveaxis(m + jnp.log(l), -1, 2).reshape(bsz, npad, hkv, g)[:, :n]
    return out.astype(q.dtype), lse


def dilated_group(q, k, v, window, dilation, slopes):
    bsz, seq, h, hd = q.shape
    n = seq // dilation

    def split(a):
        return a.reshape(bsz, n, dilation, h, hd).transpose(0, 2, 1, 3, 4).reshape(bsz * dilation, n, h, hd)

    out, lse = banded_attention(split(q)[:, :, :, None], split(k), split(v),
                                window // (2 * dilation), dilation, slopes[:, None])
    out = out[:, :, :, 0].reshape(bsz, dilation, n, h, hd).transpose(0, 2, 1, 3, 4).reshape(bsz, seq, h, hd)
    lse = lse[..., 0].reshape(bsz, dilation, n, h).transpose(0, 2, 1, 3).reshape(bsz, seq, h)
    return out, lse


def mixer_layer(x, norm_g, w_in, w_gate, b_gate, w_proj_a, w_proj_b, w_out, sink):
    bsz, seq, _ = x.shape
    h = rms_norm(x, norm_g)
    proj = h @ w_in
    qkv_a = proj[..., :A_COLS].reshape(bsz, seq, A_N_GROUPS, 3, A_HEADS, HEAD_DIM)
    slopes_a = alibi_slopes(A_N_GROUPS * A_HEADS).reshape(A_N_GROUPS, A_HEADS)
    outs, lses = [], []
    for gi, (window, dilation) in enumerate(A_GROUPS):
        o, l = dilated_group(qkv_a[:, :, gi, 0], qkv_a[:, :, gi, 1], qkv_a[:, :, gi, 2],
                             window, dilation, slopes_a[gi])
        outs.append(o)
        lses.append(l)
    wts = jax.nn.softmax(jnp.stack(lses), axis=0)
    y_a = jnp.einsum('gbsh,gbshd->bshd', wts, jnp.stack(outs).astype(jnp.float32))
    y_a = y_a.reshape(bsz, seq, A_WIDTH).astype(x.dtype)
    pb = proj[..., A_COLS:]
    q_b = pb[..., :B_Q_WIDTH].reshape(bsz, seq, B_KV_HEADS, B_GROUP, HEAD_DIM)
    k_b = pb[..., B_Q_WIDTH:B_Q_WIDTH + B_KV_WIDTH].reshape(bsz, seq, B_KV_HEADS, HEAD_DIM)
    v_b = pb[..., B_Q_WIDTH + B_KV_WIDTH:].reshape(bsz, seq, B_KV_HEADS, HEAD_DIM)
    slopes_b = alibi_slopes(B_Q_HEADS).reshape(B_KV_HEADS, B_GROUP)
    o_b, _ = banded_attention(q_b, k_b, v_b, B_HALF_WINDOW, 1, slopes_b,
                              sink.reshape(B_KV_HEADS, B_GROUP))
    y_b = o_b.reshape(bsz, seq, B_Q_WIDTH)
    gates = jax.nn.sigmoid((h @ w_gate + b_gate).astype(jnp.float32)).astype(x.dtype)
    merged = gates[..., :D_MODEL] * (y_a @ w_proj_a) + gates[..., D_MODEL:] * (y_b @ w_proj_b)
    return x + merged @ w_out


def dense_swiglu(h, w_g, w_u, w_d):
    return (jax.nn.silu(h @ w_g) * (h @ w_u)) @ w_d


def moe_swiglu(h, w_router, w_g, w_u, w_d):
    bsz, seq, d = h.shape
    xt = h.reshape(-1, d)
    t = xt.shape[0]
    logits = (xt @ w_router).astype(jnp.float32)
    top_logit, top_e = lax.top_k(logits, TOP_K)
    gate = jax.nn.softmax(top_logit, axis=-1)
    e_flat = top_e.reshape(-1)
    tok_flat = jnp.repeat(jnp.arange(t, dtype=jnp.int32), TOP_K)
    g_flat = gate.reshape(-1)
    order = jnp.argsort(e_flat)
    e_s, tok_s, g_s = e_flat[order], tok_flat[order], g_flat[order]
    counts = jnp.bincount(e_flat, length=N_EXPERTS)
    padded = (counts + MOE_BLOCK - 1) // MOE_BLOCK * MOE_BLOCK
    pend = jnp.cumsum(padded)
    pstart = pend - padded
    ustart = jnp.cumsum(counts) - counts
    n_assign = t * TOP_K
    dest = pstart[e_s] + jnp.arange(n_assign) - ustart[e_s]
    n_blocks = -(-n_assign // MOE_BLOCK) + N_EXPERTS
    cap = n_blocks * MOE_BLOCK
    slot_tok = jnp.full((cap,), t, jnp.int32).at[dest].set(tok_s)
    slot_gate = jnp.zeros((cap,), jnp.float32).at[dest].set(g_s)
    block_e = jnp.minimum(jnp.searchsorted(pend, jnp.arange(n_blocks) * MOE_BLOCK, side='right'),
                          N_EXPERTS - 1)
    xs = jnp.concatenate([xt, jnp.zeros((1, d), xt.dtype)], axis=0)[slot_tok]
    xs = xs.reshape(n_blocks, MOE_BLOCK, d)

    def expert_block(args):
        xb, e = args
        return (jax.nn.silu(xb @ w_g[e]) * (xb @ w_u[e])) @ w_d[e]

    ys = lax.map(expert_block, (xs, block_e)).reshape(cap, d)
    out = jnp.zeros((t + 1, d), jnp.float32).at[slot_tok].add(ys.astype(jnp.float32) * slot_gate[:, None])
    return out[:t].astype(h.dtype).reshape(bsz, seq, d)


def trunk(x, norm_mix_g, w_in, w_gate, b_gate, w_proj_a, w_proj_b, w_out, sink, norm_ffn_g,
          w_ff_gate, w_ff_up, w_ff_down, w_router, w_e_gate, w_e_up, w_e_down, norm_final_g):
    for layer in range(DEPTH):
        x = mixer_layer(x, norm_mix_g[layer], w_in[layer], w_gate[layer], b_gate[layer],
                        w_proj_a[layer], w_proj_b[layer], w_out[layer], sink[layer])
        h = rms_norm(x, norm_ffn_g[layer])
        i = layer // 2
        if layer % 2 == 0:
            x = x + dense_swiglu(h, w_ff_gate[i], w_ff_up[i], w_ff_down[i])
        else:
            x = x + moe_swiglu(h, w_router[i], w_e_gate[i], w_e_up[i], w_e_down[i])
    return rms_norm(x, norm_final_g)


def setup_inputs(seed: int = 0) -> dict:
    key = jax.random.key(seed)
    ks = jax.random.split(key, 20)
    f32 = jnp.float32
    nrm = lambda k, shape, scale: jax.random.normal(k, shape, f32) * scale
    return {
        "x_prompt": nrm(ks[0], (BATCH, SEQ, D_MODEL), 1.0),
        "x_sample": nrm(ks[1], (DEC_BATCH, DEC_SEQ, D_MODEL), 1.0),
        "norm_mix_g": 1.0 + nrm(ks[2], (DEPTH, D_MODEL), 0.01),
        "w_in": nrm(ks[3], (DEPTH, D_MODEL, N_IN), D_MODEL ** -0.5),
        "w_gate": nrm(ks[4], (DEPTH, D_MODEL, 2 * D_MODEL), D_MODEL ** -0.5),
        "b_gate": nrm(ks[5], (DEPTH, 2 * D_MODEL), 0.01),
        "w_proj_a": nrm(ks[6], (DEPTH, A_WIDTH, D_MODEL), A_WIDTH ** -0.5),
        "w_proj_b": nrm(ks[7], (DEPTH, B_Q_WIDTH, D_MODEL), B_Q_WIDTH ** -0.5),
        "w_out": nrm(ks[8], (DEPTH, D_MODEL, D_MODEL), D_MODEL ** -0.5),
        "sink": nrm(ks[9], (DEPTH, B_Q_HEADS), 0.5),
        "norm_ffn_g": 1.0 + nrm(ks[10], (DEPTH, D_MODEL), 0.01),
        "w_ff_gate": nrm(ks[11], (N_DENSE, D_MODEL, D_FF), D_MODEL ** -0.5),
        "w_ff_up": nrm(ks[12], (N_DENSE, D_MODEL, D_FF), D_MODEL ** -0.5),
        "w_ff_down": nrm(ks[13], (N_DENSE, D_FF, D_MODEL), D_FF ** -0.5),
        "w_router": nrm(ks[14], (N_MOE, D_MODEL, N_EXPERTS), D_MODEL ** -0.5),
        "w_e_gate": nrm(ks[15], (N_MOE, N_EXPERTS, D_MODEL, D_FF_EXPERT), D_MODEL ** -0.5),
        "w_e_up": nrm(ks[16], (N_MOE, N_EXPERTS, D_MODEL, D_FF_EXPERT), D_MODEL ** -0.5),
        "w_e_down": nrm(ks[17], (N_MOE, N_EXPERTS, D_FF_EXPERT, D_MODEL), D_FF_EXPERT ** -0.5),
        "norm_final_g": 1.0 + nrm(ks[18], (D_MODEL,), 0.01),
    }


def reference(x_prompt, x_sample, norm_mix_g, w_in, w_gate, b_gate, w_proj_a, w_proj_b, w_out, sink,
              norm_ffn_g, w_ff_gate, w_ff_up, w_ff_down, w_router, w_e_gate, w_e_up, w_e_down,
              norm_final_g):
    y_prompt = trunk(x_prompt, norm_mix_g, w_in, w_gate, b_gate, w_proj_a, w_proj_b, w_out, sink,
                     norm_ffn_g, w_ff_gate, w_ff_up, w_ff_down, w_router, w_e_gate, w_e_up, w_e_down,
                     norm_final_g)
    y_sample = trunk(x_sample, norm_mix_g, w_in, w_gate, b_gate, w_proj_a, w_proj_b, w_out, sink,
                     norm_ffn_g, w_ff_gate, w_ff_up, w_ff_down, w_router, w_e_gate, w_e_up, w_e_down,
                     norm_final_g)
    return (y_prompt, y_sample)
```

```python
import functools

import numpy as np
import jax
import jax.numpy as jnp
from jax import lax
from jax.experimental import pallas as pl
from jax.experimental.pallas import tpu as pltpu

F32 = jnp.float32
BF16 = jnp.bfloat16

D_MODEL = 1024
HEAD_DIM = 64
A_GROUPS = ((128, 1), (512, 4), (2048, 16))
A_HEADS = 4
A_WIDTH = A_HEADS * HEAD_DIM
A_COLS = len(A_GROUPS) * 3 * A_WIDTH
B_Q_HEADS = 8
B_KV_HEADS = 2
B_GROUP = B_Q_HEADS // B_KV_HEADS
B_Q_WIDTH = B_Q_HEADS * HEAD_DIM
B_KV_WIDTH = B_KV_HEADS * HEAD_DIM
B_HALF_WINDOW = 128
N_IN = A_COLS + B_Q_WIDTH + 2 * B_KV_WIDTH
N_EXPERTS = 8
TOP_K = 2
RMS_EPS = 1e-6
NEG_INF = -1e30

LANES = 128
TOKEN_TILE = 512
ATTN_Q_TILE = 128
FF_SPLIT_DENSE = 2
FF_SPLIT_EXPERT = 4
MOE_TILE = 512
ROW_TILE = 256
VMEM_LIMIT = 52 * 1024 * 1024


def _alibi_slopes(n):
    return np.asarray(2.0 ** (-8.0 * np.arange(1, n + 1) / n), dtype=np.float32)


def _params(*sem):
    return pltpu.CompilerParams(dimension_semantics=sem, vmem_limit_bytes=VMEM_LIMIT)


def _rms(x, g):
    return x * lax.rsqrt(jnp.mean(x * x, axis=-1, keepdims=True) + RMS_EPS) * g


def _dot(a, b):
    return jnp.dot(a, b, preferred_element_type=F32)


def _proj_kernel(x_ref, g_ref, w_ref, pa_ref, qb_ref, kv_ref):
    h = _rms(x_ref[...], g_ref[...]).astype(BF16)
    for c in range(len(A_GROUPS)):
        cols = slice(c * 3 * A_WIDTH, (c + 1) * 3 * A_WIDTH)
        pa_ref[:, cols] = _dot(h, w_ref[:, cols]).astype(BF16)
    qb_ref[...] = _dot(h, w_ref[:, A_COLS:A_COLS + B_Q_WIDTH]).astype(BF16)
    kv_ref[...] = _dot(h, w_ref[:, A_COLS + B_Q_WIDTH:]).astype(BF16)


def _proj(x, g, w_in):
    t = x.shape[0]
    tm = TOKEN_TILE
    row = lambda i: (i, 0)
    fixed = lambda i: (0, 0)
    return pl.pallas_call(
        _proj_kernel,
        grid=(t // tm,),
        in_specs=[pl.BlockSpec((tm, D_MODEL), row), pl.BlockSpec((1, D_MODEL), fixed),
                  pl.BlockSpec((D_MODEL, N_IN), fixed)],
        out_specs=[pl.BlockSpec((tm, A_COLS), row), pl.BlockSpec((tm, B_Q_WIDTH), row),
                   pl.BlockSpec((tm, 2 * B_KV_WIDTH), row)],
        out_shape=[jax.ShapeDtypeStruct((t, A_COLS), BF16), jax.ShapeDtypeStruct((t, B_Q_WIDTH), BF16),
                   jax.ShapeDtypeStruct((t, 2 * B_KV_WIDTH), BF16)],
        compiler_params=_params("parallel"),
        name="proj",
    )(x, g, w_in)


def _band(q0, n, tq, kw, half_window):
    start = pl.multiple_of(jnp.clip(q0 - half_window, 0, n - kw), half_window)
    rel = (lax.broadcasted_iota(jnp.int32, (tq, kw), 0) - lax.broadcasted_iota(jnp.int32, (tq, kw), 1)
           + (q0 - start))
    dist = jnp.abs(rel)
    return start, dist <= half_window, dist.astype(F32)


def _head_scores(qh, kh, valid, dist, slope):
    s = lax.dot_general(qh, kh, (((1,), (1,)), ((), ())), preferred_element_type=F32) * (HEAD_DIM ** -0.5)
    return jnp.where(valid, s - slope * dist, NEG_INF)


def _attn_a_kernel(q_ref, k_ref, v_ref, o_ref, lse_ref, *, n, slopes):
    hw = 64
    tq = min(ATTN_Q_TILE, n)
    kw = min(n, tq + 2 * hw)
    lane = lax.broadcasted_iota(jnp.int32, (tq, LANES), 1)

    def qblock(qi, carry):
        q0 = pl.multiple_of(qi * tq, tq)
        start, valid, dist = _band(q0, n, tq, kw, hw)
        q = q_ref[0, pl.ds(q0, tq), :]
        k = k_ref[0, pl.ds(start, kw), :]
        v = v_ref[0, pl.ds(start, kw), :]
        outs = []
        lse_tile = jnp.zeros((tq, LANES), F32)
        for h in range(A_HEADS):
            cols = slice(h * HEAD_DIM, (h + 1) * HEAD_DIM)
            s = _head_scores(q[:, cols], k[:, cols], valid, dist, slopes[h])
            m = jnp.max(s, axis=-1, keepdims=True)
            p = jnp.exp(s - m)
            l = jnp.sum(p, axis=-1, keepdims=True)
            outs.append(_dot(p.astype(BF16), v[:, cols]) / l)
            lse_tile = jnp.where(lane == h, m + jnp.log(l), lse_tile)
        o_ref[0, pl.ds(q0, tq), :] = jnp.concatenate(outs, axis=-1).astype(BF16)
        lse_ref[0, pl.ds(q0, tq), :] = lse_tile
        return carry

    lax.fori_loop(0, n // tq, qblock, 0)


def _attn_a(pa, bsz, seq, gi):
    window, dil = A_GROUPS[gi]
    assert window // (2 * dil) == 64
    n = seq // dil
    blocks_per_pos = A_COLS // A_WIDTH
    slopes = _alibi_slopes(len(A_GROUPS) * A_HEADS).reshape(len(A_GROUPS), A_HEADS)[gi]
    slopes = tuple(float(s * np.float32(dil)) for s in slopes)
    view = pa.reshape(bsz, n, dil * A_COLS)

    def col(which):
        return lambda b, r: (b, 0, r * blocks_per_pos + gi * 3 + which)

    out, lse = pl.pallas_call(
        functools.partial(_attn_a_kernel, n=n, slopes=slopes),
        grid=(bsz, dil),
        in_specs=[pl.BlockSpec((1, n, A_WIDTH), col(0)), pl.BlockSpec((1, n, A_WIDTH), col(1)),
                  pl.BlockSpec((1, n, A_WIDTH), col(2))],
        out_specs=[pl.BlockSpec((1, n, A_WIDTH), lambda b, r: (b, 0, r)),
                   pl.BlockSpec((1, n, LANES), lambda b, r: (b, 0, r))],
        out_shape=[jax.ShapeDtypeStruct((bsz, n, dil * A_WIDTH), BF16),
                   jax.ShapeDtypeStruct((bsz, n, dil * LANES), F32)],
        compiler_params=_params("parallel", "parallel"),
        name=f"attn_a{gi}",
    )(view, view, view)
    return out.reshape(bsz * seq, A_WIDTH), lse.reshape(bsz * seq, LANES)


def _attn_b_kernel(sink_ref, q_ref, kv_ref, o_ref, *, n, slopes):
    hw = B_HALF_WINDOW
    tq = min(ATTN_Q_TILE, n)
    kw = min(n, tq + 2 * hw)

    def qblock(qi, carry):
        q0 = pl.multiple_of(qi * tq, tq)
        start, valid, dist = _band(q0, n, tq, kw, hw)
        q = q_ref[0, pl.ds(q0, tq), :]
        kv = kv_ref[0, pl.ds(start, kw), :]
        outs = []
        for hq in range(B_Q_HEADS):
            hk = hq // B_GROUP
            kh = kv[:, hk * HEAD_DIM:(hk + 1) * HEAD_DIM]
            vh = kv[:, B_KV_WIDTH + hk * HEAD_DIM:B_KV_WIDTH + (hk + 1) * HEAD_DIM]
            s = _head_scores(q[:, hq * HEAD_DIM:(hq + 1) * HEAD_DIM], kh, valid, dist, slopes[hq])
            sk = sink_ref[hq]
            m = jnp.maximum(jnp.max(s, axis=-1, keepdims=True), sk)
            p = jnp.exp(s - m)
            l = jnp.sum(p, axis=-1, keepdims=True) + jnp.exp(sk - m)
            outs.append(_dot(p.astype(BF16), vh) / l)
        o_ref[0, pl.ds(q0, tq), :] = jnp.concatenate(outs, axis=-1).astype(BF16)
        return carry

    lax.fori_loop(0, n // tq, qblock, 0)


def _attn_b(qb, kvb, sink, bsz, seq):
    slopes = tuple(float(s) for s in _alibi_slopes(B_Q_HEADS))
    out = pl.pallas_call(
        functools.partial(_attn_b_kernel, n=seq, slopes=slopes),
        grid=(bsz,),
        in_specs=[pl.BlockSpec(memory_space=pltpu.SMEM),
                  pl.BlockSpec((1, seq, B_Q_WIDTH), lambda b: (b, 0, 0)),
                  pl.BlockSpec((1, seq, 2 * B_KV_WIDTH), lambda b: (b, 0, 0))],
        out_specs=pl.BlockSpec((1, seq, B_Q_WIDTH), lambda b: (b, 0, 0)),
        out_shape=jax.ShapeDtypeStruct((bsz, seq, B_Q_WIDTH), BF16),
        compiler_params=_params("parallel"),
        name="attn_b",
    )(sink, qb.reshape(bsz, seq, B_Q_WIDTH), kvb.reshape(bsz, seq, 2 * B_KV_WIDTH))
    return out.reshape(bsz * seq, B_Q_WIDTH)


def _merge_kernel(x_ref, o0_ref, o1_ref, o2_ref, l0_ref, l1_ref, l2_ref, yb_ref, gmix_ref, wgate_ref, bgate_ref,
                  wpa_ref, wpb_ref, wout_ref, gffn_ref, xo_ref, h2_ref):
    x = x_ref[...]
    h = _rms(x, gmix_ref[...]).astype(BF16)
    lses = (l0_ref[...], l1_ref[...], l2_ref[...])
    mx = jnp.maximum(jnp.maximum(lses[0], lses[1]), lses[2])
    es = [jnp.exp(l - mx) for l in lses]
    den = es[0] + es[1] + es[2]
    outs = (o0_ref[...], o1_ref[...], o2_ref[...])
    heads = []
    for hh in range(A_HEADS):
        cols = slice(hh * HEAD_DIM, (hh + 1) * HEAD_DIM)
        acc = None
        for g in range(len(A_GROUPS)):
            term = (es[g][:, hh:hh + 1] / den[:, hh:hh + 1]) * outs[g][:, cols].astype(F32)
            acc = term if acc is None else acc + term
        heads.append(acc)
    y_a = jnp.concatenate(heads, axis=-1).astype(BF16)
    g_a = jax.nn.sigmoid(_dot(h, wgate_ref[:, :D_MODEL]) + bgate_ref[:, :D_MODEL])
    merged = g_a * _dot(y_a, wpa_ref[...])
    g_b = jax.nn.sigmoid(_dot(h, wgate_ref[:, D_MODEL:]) + bgate_ref[:, D_MODEL:])
    merged = merged + g_b * _dot(yb_ref[...], wpb_ref[...])
    xn = x + _dot(merged.astype(BF16), wout_ref[...])
    xo_ref[...] = xn
    h2_ref[...] = _rms(xn, gffn_ref[...]).astype(h2_ref.dtype)


def _merge(x, outs, lses, y_b, gmix, w_gate, b_gate, w_pa, w_pb, w_out, gffn, h2_dtype):
    t = x.shape[0]
    tm = TOKEN_TILE
    row = lambda i: (i, 0)
    fixed = lambda i: (0, 0)
    full = lambda a: pl.BlockSpec(a.shape, fixed)
    return pl.pallas_call(
        _merge_kernel,
        grid=(t // tm,),
        in_specs=[pl.BlockSpec((tm, D_MODEL), row)]
                 + [pl.BlockSpec((tm, A_WIDTH), row)] * 3 + [pl.BlockSpec((tm, LANES), row)] * 3
                 + [pl.BlockSpec((tm, B_Q_WIDTH), row)]
                 + [full(a) for a in (gmix, w_gate, b_gate, w_pa, w_pb, w_out, gffn)],
        out_specs=[pl.BlockSpec((tm, D_MODEL), row), pl.BlockSpec((tm, D_MODEL), row)],
        out_shape=[jax.ShapeDtypeStruct((t, D_MODEL), F32), jax.ShapeDtypeStruct((t, D_MODEL), h2_dtype)],
        compiler_params=_params("parallel"),
        name="merge",
    )(x, *outs, *lses, y_b, gmix, w_gate, b_gate, w_pa, w_pb, w_out, gffn)


def _swiglu_step(xb, wg, wu, wd):
    a = _dot(xb, wg)
    return _dot((jax.nn.silu(a) * _dot(xb, wu)).astype(BF16), wd)


def _ffn_kernel(h_ref, x_ref, wg_ref, wu_ref, wd_ref, o_ref, acc_ref):
    j = pl.program_id(1)
    part = _swiglu_step(h_ref[...], wg_ref[...], wu_ref[...], wd_ref[...])

    @pl.when(j == 0)
    def _():
        acc_ref[...] = x_ref[...] + part

    @pl.when(j > 0)
    def _():
        acc_ref[...] += part

    @pl.when(j == pl.num_programs(1) - 1)
    def _():
        o_ref[...] = acc_ref[...]


def _ffn(h2, x, w_g, w_u, w_d):
    t = x.shape[0]
    tm = TOKEN_TILE
    d_ff = w_g.shape[1]
    tf = d_ff // FF_SPLIT_DENSE
    assert tf * FF_SPLIT_DENSE == d_ff and tf % LANES == 0
    row = lambda i, j: (i, 0)
    return pl.pallas_call(
        _ffn_kernel,
        grid=(t // tm, FF_SPLIT_DENSE),
        in_specs=[pl.BlockSpec((tm, D_MODEL), row), pl.BlockSpec((tm, D_MODEL), row),
                  pl.BlockSpec((D_MODEL, tf), lambda i, j: (0, j)), pl.BlockSpec((D_MODEL, tf), lambda i, j: (0, j)),
                  pl.BlockSpec((tf, D_MODEL), lambda i, j: (j, 0))],
        out_specs=pl.BlockSpec((tm, D_MODEL), row),
        out_shape=jax.ShapeDtypeStruct((t, D_MODEL), F32),
        scratch_shapes=[pltpu.VMEM((tm, D_MODEL), F32)],
        compiler_params=_params("parallel", "arbitrary"),
        name="ffn_dense",
    )(h2, x, w_g, w_u, w_d)


def _router_kernel(h_ref, wr_ref, o_ref):
    logits = jnp.dot(h_ref[...], wr_ref[...], preferred_element_type=F32, precision=lax.Precision.HIGHEST)
    lane = lax.broadcasted_iota(jnp.int32, logits.shape, 1)
    l1 = jnp.where(lane < N_EXPERTS, logits, -jnp.inf)
    m1 = jnp.max(l1, axis=-1, keepdims=True)
    i1 = jnp.min(jnp.where(l1 == m1, lane, LANES), axis=-1, keepdims=True)
    l2 = jnp.where(lane == i1, -jnp.inf, l1)
    m2 = jnp.max(l2, axis=-1, keepdims=True)
    i2 = jnp.min(jnp.where(l2 == m2, lane, LANES), axis=-1, keepdims=True)
    e2 = jnp.exp(m2 - m1)
    den = 1.0 + e2
    out = jnp.where(lane == 0, i1.astype(F32),
                    jnp.where(lane == 1, i2.astype(F32),
                              jnp.where(lane == 2, 1.0 / den, jnp.where(lane == 3, e2 / den, 0.0))))
    o_ref[...] = out


def _router(h2, w_router):
    t = h2.shape[0]
    tm = TOKEN_TILE
    wr = jnp.zeros((D_MODEL, LANES), F32).at[:, :N_EXPERTS].set(w_router)
    return pl.pallas_call(
        _router_kernel,
        grid=(t // tm,),
        in_specs=[pl.BlockSpec((tm, D_MODEL), lambda i: (i, 0)), pl.BlockSpec((D_MODEL, LANES), lambda i: (0, 0))],
        out_specs=pl.BlockSpec((tm, LANES), lambda i: (i, 0)),
        out_shape=jax.ShapeDtypeStruct((t, LANES), F32),
        compiler_params=_params("parallel"),
        name="router",
    )(h2, wr)


def _row_copy_wait(src, dst, sem, count):
    def body(r, c):
        pltpu.make_async_copy(src.at[pl.ds(0, 1), :], dst.at[pl.ds(0, 1), :], sem).wait()
        return c
    lax.fori_loop(0, count, body, 0)


def _dispatch_kernel(dest_ref, h_ref, xs_in_ref, xs_ref, sem):
    del xs_in_ref
    tm = h_ref.shape[0]

    def body(r, c):
        for k in range(TOP_K):
            d = dest_ref[0, 0, TOP_K * r + k]
            pltpu.make_async_copy(h_ref.at[pl.ds(r, 1), :], xs_ref.at[pl.ds(d, 1), :], sem).start()
        return c

    lax.fori_loop(0, tm, body, 0)
    _row_copy_wait(h_ref, xs_ref, sem, TOP_K * tm)


def _dispatch(h2, dest, cap):
    t = h2.shape[0]
    tm = ROW_TILE
    return pl.pallas_call(
        _dispatch_kernel,
        grid=(t // tm,),
        in_specs=[pl.BlockSpec((1, 1, TOP_K * tm), lambda i: (i, 0, 0), memory_space=pltpu.SMEM),
                  pl.BlockSpec((tm, D_MODEL), lambda i: (i, 0)),
                  pl.BlockSpec(memory_space=pl.ANY)],
        out_specs=pl.BlockSpec(memory_space=pl.ANY),
        out_shape=jax.ShapeDtypeStruct((cap, D_MODEL), F32),
        scratch_shapes=[pltpu.SemaphoreType.DMA(())],
        input_output_aliases={2: 0},
        compiler_params=_params("arbitrary"),
        name="dispatch",
    )(dest.reshape(t // tm, 1, TOP_K * tm), h2, jnp.zeros((cap, D_MODEL), F32))


def _expert_kernel(be_ref, xs_ref, wg_ref, wu_ref, wd_ref, ys_ref, xb_ref, acc_ref):
    del be_ref
    j = pl.program_id(1)

    @pl.when(j == 0)
    def _():
        xb_ref[...] = xs_ref[...].astype(BF16)

    part = _swiglu_step(xb_ref[...], wg_ref[0], wu_ref[0], wd_ref[0])

    @pl.when(j == 0)
    def _():
        acc_ref[...] = part

    @pl.when(j > 0)
    def _():
        acc_ref[...] += part

    @pl.when(j == pl.num_programs(1) - 1)
    def _():
        ys_ref[...] = acc_ref[...]


def _experts(xs, block_e, w_g, w_u, w_d):
    cap = xs.shape[0]
    tm = MOE_TILE
    d_ff = w_g.shape[2]
    tf = d_ff // FF_SPLIT_EXPERT
    assert tf * FF_SPLIT_EXPERT == d_ff and tf % LANES == 0
    return pl.pallas_call(
        _expert_kernel,
        grid_spec=pltpu.PrefetchScalarGridSpec(
            num_scalar_prefetch=1,
            grid=(cap // tm, FF_SPLIT_EXPERT),
            in_specs=[pl.BlockSpec((tm, D_MODEL), lambda i, j, be: (i, 0)),
                      pl.BlockSpec((1, D_MODEL, tf), lambda i, j, be: (be[i], 0, j)),
                      pl.BlockSpec((1, D_MODEL, tf), lambda i, j, be: (be[i], 0, j)),
                      pl.BlockSpec((1, tf, D_MODEL), lambda i, j, be: (be[i], j, 0))],
            out_specs=pl.BlockSpec((tm, D_MODEL), lambda i, j, be: (i, 0)),
            scratch_shapes=[pltpu.VMEM((tm, D_MODEL), BF16), pltpu.VMEM((tm, D_MODEL), F32)],
        ),
        out_shape=jax.ShapeDtypeStruct((cap, D_MODEL), F32),
        compiler_params=_params("parallel", "arbitrary"),
        name="experts",
    )(block_e, xs, w_g, w_u, w_d)


def _combine_kernel(dest_ref, r_ref, x_ref, ys_ref, g_ref, o_ref, y0_ref, y1_ref, sem):
    tm = x_ref.shape[0]
    bufs = (y0_ref, y1_ref)

    def body(r, c):
        for k in range(TOP_K):
            d = dest_ref[0, 0, TOP_K * r + k]
            pltpu.make_async_copy(ys_ref.at[pl.ds(d, 1), :], bufs[k].at[pl.ds(r, 1), :], sem).start()
        return c

    lax.fori_loop(0, tm, body, 0)
    _row_copy_wait(ys_ref, y0_ref, sem, TOP_K * tm)
    gates = r_ref[...]
    moe = gates[:, 2:3] * y0_ref[...] + gates[:, 3:4] * y1_ref[...]
    o_ref[...] = _rms(x_ref[...] + moe, g_ref[...])


def _combine(ys, dest, rout, x, g_final):
    t = x.shape[0]
    tm = ROW_TILE
    return pl.pallas_call(
        _combine_kernel,
        grid=(t // tm,),
        in_specs=[pl.BlockSpec((1, 1, TOP_K * tm), lambda i: (i, 0, 0), memory_space=pltpu.SMEM),
                  pl.BlockSpec((tm, LANES), lambda i: (i, 0)),
                  pl.BlockSpec((tm, D_MODEL), lambda i: (i, 0)),
                  pl.BlockSpec(memory_space=pl.ANY),
                  pl.BlockSpec((1, D_MODEL), lambda i: (0, 0))],
        out_specs=pl.BlockSpec((tm, D_MODEL), lambda i: (i, 0)),
        out_shape=jax.ShapeDtypeStruct((t, D_MODEL), F32),
        scratch_shapes=[pltpu.VMEM((tm, D_MODEL), F32), pltpu.VMEM((tm, D_MODEL), F32),
                        pltpu.SemaphoreType.DMA(())],
        compiler_params=_params("arbitrary"),
        name="combine",
    )(dest.reshape(t // tm, 1, TOP_K * tm), rout, x, ys, g_final)


def _routing_tables(rout, t):
    tm = MOE_TILE
    e_flat = rout[:, :TOP_K].astype(jnp.int32).reshape(-1)
    onehot = (e_flat[:, None] == jnp.arange(N_EXPERTS, dtype=jnp.int32)[None, :]).astype(jnp.int32)
    csum = jnp.cumsum(onehot, axis=0)
    counts = csum[-1]
    padded = (counts + tm - 1) // tm * tm
    pend = jnp.cumsum(padded)
    pstart = pend - padded
    dest = jnp.sum(onehot * (csum - 1 + pstart[None, :]), axis=1).astype(jnp.int32)
    n_blocks = -(-(t * TOP_K) // tm) + N_EXPERTS
    block_e = jnp.minimum(jnp.searchsorted(pend, jnp.arange(n_blocks, dtype=jnp.int32) * tm, side='right'),
                          N_EXPERTS - 1).astype(jnp.int32)
    return dest, block_e, n_blocks * tm


def _trunk(x3, wts):
    bsz, seq, _ = x3.shape
    x = x3.reshape(bsz * seq, D_MODEL)
    depth = len(wts["layers"])
    for li, lw in enumerate(wts["layers"]):
        pa, qb, kvb = _proj(x, lw["norm_mix_g"], lw["w_in"])
        outs, lses = zip(*[_attn_a(pa, bsz, seq, gi) for gi in range(len(A_GROUPS))])
        y_b = _attn_b(qb, kvb, lw["sink"], bsz, seq)
        moe = "w_router" in lw
        x, h2 = _merge(x, outs, lses, y_b, lw["norm_mix_g"], lw["w_gate"], lw["b_gate"], lw["w_proj_a"],
                       lw["w_proj_b"], lw["w_out"], lw["norm_ffn_g"], F32 if moe else BF16)
        if not moe:
            assert li < depth - 1
            x = _ffn(h2, x, lw["w_ff_gate"], lw["w_ff_up"], lw["w_ff_down"])
        else:
            assert li == depth - 1
            rout = _router(h2, lw["w_router"])
            dest, block_e, cap = _routing_tables(rout, bsz * seq)
            xs = _dispatch(h2, dest, cap)
            ys = _experts(xs, block_e, lw["w_e_gate"], lw["w_e_up"], lw["w_e_down"])
            x = _combine(ys, dest, rout, x, wts["norm_final_g"])
    return x.reshape(bsz, seq, D_MODEL)


def kernel(x_prompt, x_sample, norm_mix_g, w_in, w_gate, b_gate, w_proj_a, w_proj_b, w_out, sink, norm_ffn_g,
           w_ff_gate, w_ff_up, w_ff_down, w_router, w_e_gate, w_e_up, w_e_down, norm_final_g):
    depth = w_in.shape[0]
    layers = []
    for li in range(depth):
        lw = {
            "norm_mix_g": norm_mix_g[li].reshape(1, D_MODEL), "w_in": w_in[li].astype(BF16),
            "w_gate": w_gate[li].astype(BF16), "b_gate": b_gate[li].reshape(1, 2 * D_MODEL),
            "w_proj_a": w_proj_a[li].astype(BF16), "w_proj_b": w_proj_b[li].astype(BF16),
            "w_out": w_out[li].astype(BF16), "sink": sink[li], "norm_ffn_g": norm_ffn_g[li].reshape(1, D_MODEL),
        }
        i = li // 2
        if li % 2 == 0:
            lw.update(w_ff_gate=w_ff_gate[i].astype(BF16), w_ff_up=w_ff_up[i].astype(BF16),
                      w_ff_down=w_ff_down[i].astype(BF16))
        else:
            lw.update(w_router=w_router[i], w_e_gate=w_e_gate[i].astype(BF16), w_e_up=w_e_up[i].astype(BF16),
                      w_e_down=w_e_down[i].astype(BF16))
        layers.append(lw)
    wts = {"layers": layers, "norm_final_g": norm_final_g.reshape(1, D_MODEL)}
    return _trunk(x_prompt, wts), _trunk(x_sample, wts)
```

```python
import functools

import numpy as np
import jax
import jax.numpy as jnp
from jax import lax
from jax.experimental import pallas as pl
from jax.experimental.pallas import tpu as pltpu

F32 = jnp.float32
BF16 = jnp.bfloat16

D_MODEL = 1024
HEAD_DIM = 64
A_GROUPS = ((128, 1), (512, 4), (2048, 16))
A_HALF_WINDOW = 64
A_HEADS = 4
A_WIDTH = A_HEADS * HEAD_DIM
A_COLS = len(A_GROUPS) * 3 * A_WIDTH
B_Q_HEADS = 8
B_KV_HEADS = 2
B_GROUP = B_Q_HEADS // B_KV_HEADS
B_Q_WIDTH = B_Q_HEADS * HEAD_DIM
B_KV_WIDTH = B_KV_HEADS * HEAD_DIM
B_HALF_WINDOW = 128
N_IN = A_COLS + B_Q_WIDTH + 2 * B_KV_WIDTH
N_EXPERTS = 8
TOP_K = 2
RMS_EPS = 1e-6
NEG_INF = -1e30

LANES = 128
TOKEN_TILE = 512
ATTN_Q_TILE = 128
FF_SPLIT_DENSE = 2
FF_SPLIT_EXPERT = 4
MOE_TILE = 512
DISPATCH_TILE = 1024
COMBINE_TILE = 512
VMEM_LIMIT = 52 * 1024 * 1024


def _alibi_slopes(n):
    return np.asarray(2.0 ** (-8.0 * np.arange(1, n + 1) / n), dtype=np.float32)


def _params(*sem):
    return pltpu.CompilerParams(dimension_semantics=sem, vmem_limit_bytes=VMEM_LIMIT)


def _rms(x, g):
    return x * lax.rsqrt(jnp.mean(x * x, axis=-1, keepdims=True) + RMS_EPS) * g


def _dot(a, b):
    return jnp.dot(a, b, preferred_element_type=F32)


def _residue_spec(seq, dil, width):
    tiles_per_seq = seq // TOKEN_TILE
    return pl.BlockSpec((1, dil, TOKEN_TILE // dil, width),
                        lambda i: (i // tiles_per_seq, 0, i % tiles_per_seq, 0))


def _proj_kernel(x_ref, g_ref, w_ref, *refs):
    pa_refs, (qb_ref, kv_ref, scr_ref) = refs[:len(A_GROUPS)], refs[len(A_GROUPS):]
    h = _rms(x_ref[...], g_ref[...]).astype(BF16)
    for gi, (_, dil) in enumerate(A_GROUPS):
        cols = slice(gi * 3 * A_WIDTH, (gi + 1) * 3 * A_WIDTH)
        res = _dot(h, w_ref[:, cols])
        if dil == 1:
            pa_refs[gi][0, 0] = res.astype(BF16)
        else:
            chunks = res.shape[1] // LANES
            for c in range(chunks):
                scr_ref[c] = res[:, c * LANES:(c + 1) * LANES]
            rows = TOKEN_TILE // dil
            for r in range(dil):
                parts = [scr_ref[c, pl.ds(r, rows, stride=dil), :] for c in range(chunks)]
                pa_refs[gi][0, r] = jnp.concatenate(parts, axis=-1).astype(BF16)
    qb_ref[...] = _dot(h, w_ref[:, A_COLS:A_COLS + B_Q_WIDTH]).astype(BF16)
    kv_ref[...] = _dot(h, w_ref[:, A_COLS + B_Q_WIDTH:]).astype(BF16)


def _proj(x, g, w_in, bsz, seq):
    t = x.shape[0]
    tm = TOKEN_TILE
    row = lambda i: (i, 0)
    fixed = lambda i: (0, 0)
    gw = 3 * A_WIDTH
    return pl.pallas_call(
        _proj_kernel,
        grid=(t // tm,),
        in_specs=[pl.BlockSpec((tm, D_MODEL), row), pl.BlockSpec((1, D_MODEL), fixed),
                  pl.BlockSpec((D_MODEL, N_IN), fixed)],
        out_specs=[_residue_spec(seq, dil, gw) for _, dil in A_GROUPS]
                  + [pl.BlockSpec((tm, B_Q_WIDTH), row), pl.BlockSpec((tm, 2 * B_KV_WIDTH), row)],
        out_shape=[jax.ShapeDtypeStruct((bsz, dil, seq // dil, gw), BF16) for _, dil in A_GROUPS]
                  + [jax.ShapeDtypeStruct((t, B_Q_WIDTH), BF16), jax.ShapeDtypeStruct((t, 2 * B_KV_WIDTH), BF16)],
        scratch_shapes=[pltpu.VMEM((gw // LANES, tm, LANES), F32)],
        compiler_params=_params("parallel"),
        name="proj",
    )(x, g, w_in)


def _band(q0, n, tq, kw, half_window):
    start = pl.multiple_of(jnp.clip(q0 - half_window, 0, n - kw), half_window)
    rel = (lax.broadcasted_iota(jnp.int32, (tq, kw), 0) - lax.broadcasted_iota(jnp.int32, (tq, kw), 1)
           + (q0 - start))
    dist = jnp.abs(rel)
    return start, dist <= half_window, dist.astype(F32)


def _head_scores(qh, kh, valid, dist, slope):
    s = lax.dot_general(qh, kh, (((1,), (1,)), ((), ())), preferred_element_type=F32) * (HEAD_DIM ** -0.5)
    return jnp.where(valid, s - slope * dist, NEG_INF)


def _attn_a_kernel(x_ref, o_ref, lse_ref, *, n, slopes):
    hw = A_HALF_WINDOW
    tq = min(ATTN_Q_TILE, n)
    kw = min(n, tq + 2 * hw)
    lane = lax.broadcasted_iota(jnp.int32, (tq, LANES), 1)

    def qblock(qi, carry):
        q0 = pl.multiple_of(qi * tq, tq)
        start, valid, dist = _band(q0, n, tq, kw, hw)
        q = x_ref[0, 0, pl.ds(q0, tq), 0:A_WIDTH]
        k = x_ref[0, 0, pl.ds(start, kw), A_WIDTH:2 * A_WIDTH]
        v = x_ref[0, 0, pl.ds(start, kw), 2 * A_WIDTH:3 * A_WIDTH]
        outs = []
        lse_tile = jnp.zeros((tq, LANES), F32)
        for h in range(A_HEADS):
            cols = slice(h * HEAD_DIM, (h + 1) * HEAD_DIM)
            s = _head_scores(q[:, cols], k[:, cols], valid, dist, slopes[h])
            m = jnp.max(s, axis=-1, keepdims=True)
            p = jnp.exp(s - m)
            l = jnp.sum(p, axis=-1, keepdims=True)
            outs.append(_dot(p.astype(BF16), v[:, cols]) / l)
            lse_tile = jnp.where(lane == h, m + jnp.log(l), lse_tile)
        o_ref[0, 0, pl.ds(q0, tq), :] = jnp.concatenate(outs, axis=-1).astype(BF16)
        lse_ref[0, 0, pl.ds(q0, tq), :] = lse_tile
        return carry

    lax.fori_loop(0, n // tq, qblock, 0)


def _attn_a(pa, gi):
    window, dil = A_GROUPS[gi]
    assert window // (2 * dil) == A_HALF_WINDOW
    bsz, _, n, gw = pa.shape
    slopes = _alibi_slopes(len(A_GROUPS) * A_HEADS).reshape(len(A_GROUPS), A_HEADS)[gi]
    slopes = tuple(float(s * np.float32(dil)) for s in slopes)
    block = lambda w: pl.BlockSpec((1, 1, n, w), lambda b, r: (b, r, 0, 0))
    return pl.pallas_call(
        functools.partial(_attn_a_kernel, n=n, slopes=slopes),
        grid=(bsz, dil),
        in_specs=[block(gw)],
        out_specs=[block(A_WIDTH), block(LANES)],
        out_shape=[jax.ShapeDtypeStruct((bsz, dil, n, A_WIDTH), BF16),
                   jax.ShapeDtypeStruct((bsz, dil, n, LANES), F32)],
        compiler_params=_params("parallel", "parallel"),
        name=f"attn_a{gi}",
    )(pa)


def _attn_b_kernel(sink_ref, q_ref, kv_ref, o_ref, *, n, slopes):
    hw = B_HALF_WINDOW
    tq = min(ATTN_Q_TILE, n)
    kw = min(n, tq + 2 * hw)

    def qblock(qi, carry):
        q0 = pl.multiple_of(qi * tq, tq)
        start, valid, dist = _band(q0, n, tq, kw, hw)
        q = q_ref[0, pl.ds(q0, tq), :]
        kv = kv_ref[0, pl.ds(start, kw), :]
        outs = []
        for hq in range(B_Q_HEADS):
            hk = hq // B_GROUP
            kh = kv[:, hk * HEAD_DIM:(hk + 1) * HEAD_DIM]
            vh = kv[:, B_KV_WIDTH + hk * HEAD_DIM:B_KV_WIDTH + (hk + 1) * HEAD_DIM]
            s = _head_scores(q[:, hq * HEAD_DIM:(hq + 1) * HEAD_DIM], kh, valid, dist, slopes[hq])
            sk = sink_ref[hq]
            m = jnp.maximum(jnp.max(s, axis=-1, keepdims=True), sk)
            p = jnp.exp(s - m)
            l = jnp.sum(p, axis=-1, keepdims=True) + jnp.exp(sk - m)
            outs.append(_dot(p.astype(BF16), vh) / l)
        o_ref[0, pl.ds(q0, tq), :] = jnp.concatenate(outs, axis=-1).astype(BF16)
        return carry

    lax.fori_loop(0, n // tq, qblock, 0)


def _attn_b(qb, kvb, sink, bsz, seq):
    slopes = tuple(float(s) for s in _alibi_slopes(B_Q_HEADS))
    out = pl.pallas_call(
        functools.partial(_attn_b_kernel, n=seq, slopes=slopes),
        grid=(bsz,),
        in_specs=[pl.BlockSpec(memory_space=pltpu.SMEM),
                  pl.BlockSpec((1, seq, B_Q_WIDTH), lambda b: (b, 0, 0)),
                  pl.BlockSpec((1, seq, 2 * B_KV_WIDTH), lambda b: (b, 0, 0))],
        out_specs=pl.BlockSpec((1, seq, B_Q_WIDTH), lambda b: (b, 0, 0)),
        out_shape=jax.ShapeDtypeStruct((bsz, seq, B_Q_WIDTH), BF16),
        compiler_params=_params("parallel"),
        name="attn_b",
    )(sink, qb.reshape(bsz, seq, B_Q_WIDTH), kvb.reshape(bsz, seq, 2 * B_KV_WIDTH))
    return out.reshape(bsz * seq, B_Q_WIDTH)


def _split_bf16(a):
    hi = a.astype(BF16)
    return hi, (a - hi.astype(F32)).astype(BF16)


def _top2(h2, wr_hi, wr_lo):
    h_hi, h_lo = _split_bf16(h2)
    logits = _dot(h_hi, wr_hi) + (_dot(h_hi, wr_lo) + _dot(h_lo, wr_hi))
    lane = lax.broadcasted_iota(jnp.int32, logits.shape, 1)
    l1 = jnp.where(lane < N_EXPERTS, logits, -jnp.inf)
    m1 = jnp.max(l1, axis=-1, keepdims=True)
    i1 = jnp.min(jnp.where(l1 == m1, lane, LANES), axis=-1, keepdims=True)
    l2 = jnp.where(lane == i1, -jnp.inf, l1)
    m2 = jnp.max(l2, axis=-1, keepdims=True)
    i2 = jnp.min(jnp.where(l2 == m2, lane, LANES), axis=-1, keepdims=True)
    e2 = jnp.exp(m2 - m1)
    den = 1.0 + e2
    return jnp.where(lane == 0, i1.astype(F32),
                     jnp.where(lane == 1, i2.astype(F32),
                               jnp.where(lane == 2, 1.0 / den, jnp.where(lane == 3, e2 / den, 0.0))))


def _merge_kernel(*refs, moe):
    ng = len(A_GROUPS)
    x_ref = refs[0]
    o_refs, l_refs = refs[1:1 + ng], refs[1 + ng:1 + 2 * ng]
    yb_ref, gmix_ref, wgate_ref, bgate_ref, wpa_ref, wpb_ref, wout_ref, gffn_ref = refs[1 + 2 * ng:9 + 2 * ng]
    rest = refs[9 + 2 * ng:]
    if moe:
        wr_hi_ref, wr_lo_ref, xo_ref, h2_ref, rout_ref, so_ref, sl_ref = rest
    else:
        xo_ref, h2_ref, so_ref, sl_ref = rest
    x = x_ref[...]
    h = _rms(x, gmix_ref[...]).astype(BF16)
    outs, lses = [], []
    for gi, (_, dil) in enumerate(A_GROUPS):
        if dil == 1:
            outs.append(o_refs[gi][0, 0].astype(F32))
            lses.append(l_refs[gi][0, 0])
        else:
            rows = TOKEN_TILE // dil
            chunks = A_WIDTH // LANES
            for r in range(dil):
                o_r = o_refs[gi][0, r].astype(F32)
                for c in range(chunks):
                    so_ref[gi, c, pl.ds(r, rows, stride=dil), :] = o_r[:, c * LANES:(c + 1) * LANES]
                sl_ref[gi, pl.ds(r, rows, stride=dil), :] = l_refs[gi][0, r]
            outs.append(jnp.concatenate([so_ref[gi, c] for c in range(chunks)], axis=-1))
            lses.append(sl_ref[gi])
    mx = jnp.maximum(jnp.maximum(lses[0], lses[1]), lses[2])
    es = [jnp.exp(l - mx) for l in lses]
    den = es[0] + es[1] + es[2]
    heads = []
    for hh in range(A_HEADS):
        cols = slice(hh * HEAD_DIM, (hh + 1) * HEAD_DIM)
        acc = None
        for g in range(ng):
            term = (es[g][:, hh:hh + 1] / den[:, hh:hh + 1]) * outs[g][:, cols]
            acc = term if acc is None else acc + term
        heads.append(acc)
    y_a = jnp.concatenate(heads, axis=-1).astype(BF16)
    g_a = jax.nn.sigmoid(_dot(h, wgate_ref[:, :D_MODEL]) + bgate_ref[:, :D_MODEL])
    merged = g_a * _dot(y_a, wpa_ref[...])
    g_b = jax.nn.sigmoid(_dot(h, wgate_ref[:, D_MODEL:]) + bgate_ref[:, D_MODEL:])
    merged = merged + g_b * _dot(yb_ref[...], wpb_ref[...])
    xn = x + _dot(merged.astype(BF16), wout_ref[...])
    xo_ref[...] = xn
    h2 = _rms(xn, gffn_ref[...])
    h2_ref[...] = h2.astype(h2_ref.dtype)
    if moe:
        rout_ref[...] = _top2(h2, wr_hi_ref[...], wr_lo_ref[...])


def _merge(x, outs, lses, y_b, seq, gmix, w_gate, b_gate, w_pa, w_pb, w_out, gffn, w_router=None):
    t = x.shape[0]
    tm = TOKEN_TILE
    moe = w_router is not None
    row = lambda i: (i, 0)
    fixed = lambda i: (0, 0)
    full = lambda a: pl.BlockSpec(a.shape, fixed)
    weights = [gmix, w_gate, b_gate, w_pa, w_pb, w_out, gffn]
    out_specs = [pl.BlockSpec((tm, D_MODEL), row), pl.BlockSpec((tm, D_MODEL), row)]
    out_shape = [jax.ShapeDtypeStruct((t, D_MODEL), F32), jax.ShapeDtypeStruct((t, D_MODEL), F32 if moe else BF16)]
    if moe:
        weights.extend(_split_bf16(jnp.zeros((D_MODEL, LANES), F32).at[:, :N_EXPERTS].set(w_router)))
        out_specs.append(pl.BlockSpec((tm, LANES), row))
        out_shape.append(jax.ShapeDtypeStruct((t, LANES), F32))
    ng = len(A_GROUPS)
    return pl.pallas_call(
        functools.partial(_merge_kernel, moe=moe),
        grid=(t // tm,),
        in_specs=[pl.BlockSpec((tm, D_MODEL), row)]
                 + [_residue_spec(seq, dil, A_WIDTH) for _, dil in A_GROUPS]
                 + [_residue_spec(seq, dil, LANES) for _, dil in A_GROUPS]
                 + [pl.BlockSpec((tm, B_Q_WIDTH), row)] + [full(a) for a in weights],
        out_specs=out_specs,
        out_shape=out_shape,
        scratch_shapes=[pltpu.VMEM((ng, A_WIDTH // LANES, tm, LANES), F32), pltpu.VMEM((ng, tm, LANES), F32)],
        compiler_params=_params("parallel"),
        name="merge",
    )(x, *outs, *lses, y_b, *weights)


def _swiglu_step(xb, wg, wu, wd):
    a = _dot(xb, wg)
    return _dot((jax.nn.silu(a) * _dot(xb, wu)).astype(BF16), wd)


def _ffn_kernel(h_ref, x_ref, wg_ref, wu_ref, wd_ref, o_ref, acc_ref):
    j = pl.program_id(1)
    part = _swiglu_step(h_ref[...], wg_ref[...], wu_ref[...], wd_ref[...])

    @pl.when(j == 0)
    def _():
        acc_ref[...] = x_ref[...] + part

    @pl.when(j > 0)
    def _():
        acc_ref[...] += part

    @pl.when(j == pl.num_programs(1) - 1)
    def _():
        o_ref[...] = acc_ref[...]


def _ffn(h2, x, w_g, w_u, w_d):
    t = x.shape[0]
    tm = TOKEN_TILE
    d_ff = w_g.shape[1]
    tf = d_ff // FF_SPLIT_DENSE
    assert tf * FF_SPLIT_DENSE == d_ff and tf % LANES == 0
    row = lambda i, j: (i, 0)
    return pl.pallas_call(
        _ffn_kernel,
        grid=(t // tm, FF_SPLIT_DENSE),
        in_specs=[pl.BlockSpec((tm, D_MODEL), row), pl.BlockSpec((tm, D_MODEL), row),
                  pl.BlockSpec((D_MODEL, tf), lambda i, j: (0, j)), pl.BlockSpec((D_MODEL, tf), lambda i, j: (0, j)),
                  pl.BlockSpec((tf, D_MODEL), lambda i, j: (j, 0))],
        out_specs=pl.BlockSpec((tm, D_MODEL), row),
        out_shape=jax.ShapeDtypeStruct((t, D_MODEL), F32),
        scratch_shapes=[pltpu.VMEM((tm, D_MODEL), F32)],
        compiler_params=_params("parallel", "arbitrary"),
        name="ffn_dense",
    )(h2, x, w_g, w_u, w_d)


def _dispatch_kernel(dest_ref, h_ref, xs_in_ref, xs_ref, sem):
    del xs_in_ref
    tm = h_ref.shape[0]

    def body(r, c):
        for k in range(TOP_K):
            d = dest_ref[0, 0, TOP_K * r + k]
            pltpu.make_async_copy(h_ref.at[pl.ds(r, 1), :], xs_ref.at[pl.ds(d, 1), :], sem).start()
        return c

    lax.fori_loop(0, tm, body, 0)
    for k in range(TOP_K):
        pltpu.make_async_copy(h_ref, xs_ref.at[pl.ds(0, tm), :], sem).wait()


def _dispatch(h2, dest, cap):
    t = h2.shape[0]
    tm = DISPATCH_TILE
    return pl.pallas_call(
        _dispatch_kernel,
        grid=(t // tm,),
        in_specs=[pl.BlockSpec((1, 1, TOP_K * tm), lambda i: (i, 0, 0), memory_space=pltpu.SMEM),
                  pl.BlockSpec((tm, D_MODEL), lambda i: (i, 0)),
                  pl.BlockSpec(memory_space=pl.ANY)],
        out_specs=pl.BlockSpec(memory_space=pl.ANY),
        out_shape=jax.ShapeDtypeStruct((cap, D_MODEL), F32),
        scratch_shapes=[pltpu.SemaphoreType.DMA(())],
        input_output_aliases={2: 0},
        compiler_params=_params("arbitrary"),
        name="dispatch",
    )(dest.reshape(t // tm, 1, TOP_K * tm), h2, jnp.zeros((cap, D_MODEL), F32))


def _expert_kernel(be_ref, xs_ref, wg_ref, wu_ref, wd_ref, ys_ref, xb_ref, acc_ref):
    del be_ref
    j = pl.program_id(1)

    @pl.when(j == 0)
    def _():
        xb_ref[...] = xs_ref[...].astype(BF16)

    part = _swiglu_step(xb_ref[...], wg_ref[0], wu_ref[0], wd_ref[0])

    @pl.when(j == 0)
    def _():
        acc_ref[...] = part

    @pl.when(j > 0)
    def _():
        acc_ref[...] += part

    @pl.when(j == pl.num_programs(1) - 1)
    def _():
        ys_ref[...] = acc_ref[...]


def _experts(xs, block_e, w_g, w_u, w_d):
    cap = xs.shape[0]
    tm = MOE_TILE
    d_ff = w_g.shape[2]
    tf = d_ff // FF_SPLIT_EXPERT
    assert tf * FF_SPLIT_EXPERT == d_ff and tf % LANES == 0
    return pl.pallas_call(
        _expert_kernel,
        grid_spec=pltpu.PrefetchScalarGridSpec(
            num_scalar_prefetch=1,
            grid=(cap // tm, FF_SPLIT_EXPERT),
            in_specs=[pl.BlockSpec((tm, D_MODEL), lambda i, j, be: (i, 0)),
                      pl.BlockSpec((1, D_MODEL, tf), lambda i, j, be: (be[i], 0, j)),
                      pl.BlockSpec((1, D_MODEL, tf), lambda i, j, be: (be[i], 0, j)),
                      pl.BlockSpec((1, tf, D_MODEL), lambda i, j, be: (be[i], j, 0))],
            out_specs=pl.BlockSpec((tm, D_MODEL), lambda i, j, be: (i, 0)),
            scratch_shapes=[pltpu.VMEM((tm, D_MODEL), BF16), pltpu.VMEM((tm, D_MODEL), F32)],
        ),
        out_shape=jax.ShapeDtypeStruct((cap, D_MODEL), F32),
        compiler_params=_params("parallel", "arbitrary"),
        name="experts",
    )(block_e, xs, w_g, w_u, w_d)


def _combine_kernel(dest_ref, dest_next_ref, r_ref, x_ref, ys_ref, g_ref, o_ref, y_ref, sem):
    tm = x_ref.shape[0]
    i = pl.program_id(0)

    def fetch(idx_ref, slot):
        def body(r, c):
            for k in range(TOP_K):
                d = idx_ref[0, 0, TOP_K * r + k]
                pltpu.make_async_copy(ys_ref.at[pl.ds(d, 1), :], y_ref.at[slot, k, pl.ds(r, 1), :],
                                      sem.at[slot, k]).start()
            return c
        lax.fori_loop(0, tm, body, 0)

    @pl.when(i == 0)
    def _():
        fetch(dest_ref, 0)

    @pl.when(i + 1 < pl.num_programs(0))
    def _():
        fetch(dest_next_ref, (i + 1) % 2)

    slot = i % 2
    for k in range(TOP_K):
        pltpu.make_async_copy(ys_ref.at[pl.ds(0, tm), :], y_ref.at[slot, k], sem.at[slot, k]).wait()
    gates = r_ref[...]
    moe = gates[:, 2:3] * y_ref[slot, 0] + gates[:, 3:4] * y_ref[slot, 1]
    o_ref[...] = _rms(x_ref[...] + moe, g_ref[...])


def _combine(ys, dest, rout, x, g_final):
    t = x.shape[0]
    tm = COMBINE_TILE
    steps = t // tm
    dest3 = dest.reshape(steps, 1, TOP_K * tm)
    return pl.pallas_call(
        _combine_kernel,
        grid=(steps,),
        in_specs=[pl.BlockSpec((1, 1, TOP_K * tm), lambda i: (i, 0, 0), memory_space=pltpu.SMEM),
                  pl.BlockSpec((1, 1, TOP_K * tm), lambda i: (jnp.minimum(i + 1, steps - 1), 0, 0),
                               memory_space=pltpu.SMEM),
                  pl.BlockSpec((tm, LANES), lambda i: (i, 0)),
                  pl.BlockSpec((tm, D_MODEL), lambda i: (i, 0)),
                  pl.BlockSpec(memory_space=pl.ANY),
                  pl.BlockSpec((1, D_MODEL), lambda i: (0, 0))],
        out_specs=pl.BlockSpec((tm, D_MODEL), lambda i: (i, 0)),
        out_shape=jax.ShapeDtypeStruct((t, D_MODEL), F32),
        scratch_shapes=[pltpu.VMEM((2, TOP_K, tm, D_MODEL), F32), pltpu.SemaphoreType.DMA((2, TOP_K))],
        compiler_params=_params("arbitrary"),
        name="combine",
    )(dest3, dest3, rout, x, ys, g_final)


def _routing_tables(rout, t):
    tm = MOE_TILE
    e_flat = rout[:, :TOP_K].astype(jnp.int32).reshape(-1)
    onehot = (e_flat[:, None] == jnp.arange(N_EXPERTS, dtype=jnp.int32)[None, :]).astype(jnp.int32)
    csum = jnp.cumsum(onehot, axis=0)
    counts = csum[-1]
    padded = (counts + tm - 1) // tm * tm
    pend = jnp.cumsum(padded)
    pstart = pend - padded
    dest = jnp.sum(onehot * (csum - 1 + pstart[None, :]), axis=1).astype(jnp.int32)
    n_blocks = -(-(t * TOP_K) // tm) + N_EXPERTS
    block_start = jnp.arange(n_blocks, dtype=jnp.int32) * tm
    block_e = jnp.minimum(jnp.sum((pend[None, :] <= block_start[:, None]).astype(jnp.int32), axis=1), N_EXPERTS - 1)
    return dest, block_e, n_blocks * tm


def _trunk(x3, wts):
    bsz, seq, _ = x3.shape
    assert seq % TOKEN_TILE == 0
    x = x3.reshape(bsz * seq, D_MODEL)
    depth = len(wts["layers"])
    for li, lw in enumerate(wts["layers"]):
        *pas, qb, kvb = _proj(x, lw["norm_mix_g"], lw["w_in"], bsz, seq)
        outs, lses = zip(*[_attn_a(pa, gi) for gi, pa in enumerate(pas)])
        y_b = _attn_b(qb, kvb, lw["sink"], bsz, seq)
        mixer = (lw["norm_mix_g"], lw["w_gate"], lw["b_gate"], lw["w_proj_a"], lw["w_proj_b"], lw["w_out"],
                 lw["norm_ffn_g"])
        if "w_router" not in lw:
            assert li < depth - 1
            x, h2 = _merge(x, outs, lses, y_b, seq, *mixer)
            x = _ffn(h2, x, lw["w_ff_gate"], lw["w_ff_up"], lw["w_ff_down"])
        else:
            assert li == depth - 1
            x, h2, rout = _merge(x, outs, lses, y_b, seq, *mixer, w_router=lw["w_router"])
            dest, block_e, cap = _routing_tables(rout, bsz * seq)
            xs = _dispatch(h2, dest, cap)
            ys = _experts(xs, block_e, lw["w_e_gate"], lw["w_e_up"], lw["w_e_down"])
            x = _combine(ys, dest, rout, x, wts["norm_final_g"])
    return x.reshape(bsz, seq, D_MODEL)


def kernel(x_prompt, x_sample, norm_mix_g, w_in, w_gate, b_gate, w_proj_a, w_proj_b, w_out, sink, norm_ffn_g,
           w_ff_gate, w_ff_up, w_ff_down, w_router, w_e_gate, w_e_up, w_e_down, norm_final_g):
    depth = w_in.shape[0]
    layers = []
    for li in range(depth):
        lw = {
            "norm_mix_g": norm_mix_g[li].reshape(1, D_MODEL), "w_in": w_in[li].astype(BF16),
            "w_gate": w_gate[li].astype(BF16), "b_gate": b_gate[li].reshape(1, 2 * D_MODEL),
            "w_proj_a": w_proj_a[li].astype(BF16), "w_proj_b": w_proj_b[li].astype(BF16),
            "w_out": w_out[li].astype(BF16), "sink": sink[li], "norm_ffn_g": norm_ffn_g[li].reshape(1, D_MODEL),
        }
        i = li // 2
        if li % 2 == 0:
            lw.update(w_ff_gate=w_ff_gate[i].astype(BF16), w_ff_up=w_ff_up[i].astype(BF16),
                      w_ff_down=w_ff_down[i].astype(BF16))
        else:
            lw.update(w_router=w_router[i], w_e_gate=w_e_gate[i].astype(BF16), w_e_up=w_e_up[i].astype(BF16),
                      w_e_down=w_e_down[i].astype(BF16))
        layers.append(lw)
    wts = {"layers": layers, "norm_final_g": norm_final_g.reshape(1, D_MODEL)}
    return _trunk(x_prompt, wts), _trunk(x_sample, wts)
```

```python
import functools

import numpy as np
import jax
import jax.numpy as jnp
from jax import lax
from jax.experimental import pallas as pl
from jax.experimental.pallas import tpu as pltpu

F32 = jnp.float32
BF16 = jnp.bfloat16

D_MODEL = 1024
HEAD_DIM = 64
A_GROUPS = ((128, 1), (512, 4), (2048, 16))
A_HALF_WINDOW = 64
A_HEADS = 4
A_WIDTH = A_HEADS * HEAD_DIM
A_COLS = len(A_GROUPS) * 3 * A_WIDTH
B_Q_HEADS = 8
B_KV_HEADS = 2
B_GROUP = B_Q_HEADS // B_KV_HEADS
B_Q_WIDTH = B_Q_HEADS * HEAD_DIM
B_KV_WIDTH = B_KV_HEADS * HEAD_DIM
B_HALF_WINDOW = 128
N_IN = A_COLS + B_Q_WIDTH + 2 * B_KV_WIDTH
N_EXPERTS = 8
TOP_K = 2
RMS_EPS = 1e-6
NEG_INF = -1e30

LANES = 128
TOKEN_TILE = 512
ATTN_Q_TILE = 128
ATTN_A_ROWS = 2048
ATTN_A_BLOCKS = 4
FF_SPLIT_DENSE = 2
FF_SPLIT_EXPERT = 4
MOE_TILE = 512
DISPATCH_TILE = 1024
COMBINE_TILE = 512
VMEM_LIMIT = 52 * 1024 * 1024


def _alibi_slopes(n):
    return np.asarray(2.0 ** (-8.0 * np.arange(1, n + 1) / n), dtype=np.float32)


def _params(*sem):
    return pltpu.CompilerParams(dimension_semantics=sem, vmem_limit_bytes=VMEM_LIMIT)


def _rms(x, g):
    return x * lax.rsqrt(jnp.mean(x * x, axis=-1, keepdims=True) + RMS_EPS) * g


def _dot(a, b):
    return jnp.dot(a, b, preferred_element_type=F32)


def _residue_spec(seq, dil, width):
    tiles_per_seq = seq // TOKEN_TILE
    return pl.BlockSpec((1, dil, TOKEN_TILE // dil, width),
                        lambda i: (i // tiles_per_seq, 0, i % tiles_per_seq, 0))


def _proj_kernel(x_ref, g_ref, w_ref, *refs):
    pa_refs, (qb_ref, kv_ref, scr_ref) = refs[:len(A_GROUPS)], refs[len(A_GROUPS):]
    h = _rms(x_ref[...], g_ref[...]).astype(BF16)
    for gi, (_, dil) in enumerate(A_GROUPS):
        cols = slice(gi * 3 * A_WIDTH, (gi + 1) * 3 * A_WIDTH)
        res = _dot(h, w_ref[:, cols])
        if dil == 1:
            pa_refs[gi][0, 0] = res.astype(BF16)
        else:
            chunks = res.shape[1] // LANES
            for c in range(chunks):
                scr_ref[c] = res[:, c * LANES:(c + 1) * LANES]
            rows = TOKEN_TILE // dil
            for r in range(dil):
                parts = [scr_ref[c, pl.ds(r, rows, stride=dil), :] for c in range(chunks)]
                pa_refs[gi][0, r] = jnp.concatenate(parts, axis=-1).astype(BF16)
    qb_ref[...] = _dot(h, w_ref[:, A_COLS:A_COLS + B_Q_WIDTH]).astype(BF16)
    kv_ref[...] = _dot(h, w_ref[:, A_COLS + B_Q_WIDTH:]).astype(BF16)


def _proj(x, g, w_in, bsz, seq):
    t = x.shape[0]
    tm = TOKEN_TILE
    row = lambda i: (i, 0)
    fixed = lambda i: (0, 0)
    gw = 3 * A_WIDTH
    return pl.pallas_call(
        _proj_kernel,
        grid=(t // tm,),
        in_specs=[pl.BlockSpec((tm, D_MODEL), row), pl.BlockSpec((1, D_MODEL), fixed),
                  pl.BlockSpec((D_MODEL, N_IN), fixed)],
        out_specs=[_residue_spec(seq, dil, gw) for _, dil in A_GROUPS]
                  + [pl.BlockSpec((tm, B_Q_WIDTH), row), pl.BlockSpec((tm, 2 * B_KV_WIDTH), row)],
        out_shape=[jax.ShapeDtypeStruct((bsz, dil, seq // dil, gw), BF16) for _, dil in A_GROUPS]
                  + [jax.ShapeDtypeStruct((t, B_Q_WIDTH), BF16), jax.ShapeDtypeStruct((t, 2 * B_KV_WIDTH), BF16)],
        scratch_shapes=[pltpu.VMEM((gw // LANES, tm, LANES), F32)],
        compiler_params=_params("parallel"),
        name="proj",
    )(x, g, w_in)


def _band(q0, n, tq, kw, half_window):
    start = pl.multiple_of(jnp.clip(q0 - half_window, 0, n - kw), half_window)
    rel = (lax.broadcasted_iota(jnp.int32, (tq, kw), 0) - lax.broadcasted_iota(jnp.int32, (tq, kw), 1)
           + (q0 - start))
    dist = jnp.abs(rel)
    return start, dist <= half_window, dist.astype(F32)


Q_SCALE = HEAD_DIM ** -0.5


def _head_scores(qh, kh, valid, dist, slope):
    s = lax.dot_general(qh, kh, (((1,), (1,)), ((), ())), preferred_element_type=F32)
    return jnp.where(valid, s - slope * dist, NEG_INF)


def _attn_a_shapes(n, dil):
    tq = min(ATTN_Q_TILE, n)
    res = max(1, min(dil, ATTN_A_ROWS // n))
    return tq, min(n, tq + 2 * A_HALF_WINDOW), res, min(ATTN_A_BLOCKS, res * (n // tq))


def _attn_a_kernel(x_ref, o_ref, lse_ref, s_ref, p_ref, *, n, dil, slopes):
    tq, kw, res, nb = _attn_a_shapes(n, dil)
    per_res = n // tq
    lane = lax.broadcasted_iota(jnp.int32, (tq, LANES), 1)
    pairs = [(j, h) for j in range(nb) for h in range(A_HEADS)]
    head_cols = lambda h: slice(h * HEAD_DIM, (h + 1) * HEAD_DIM)

    def step(it, carry):
        blocks = []
        for j in range(nb):
            blk = it * nb + j
            r = blk // per_res
            q0 = pl.multiple_of((blk % per_res) * tq, tq)
            start, valid, dist = _band(q0, n, tq, kw, A_HALF_WINDOW)
            blocks.append((r, q0, valid, dist, x_ref[0, r, pl.ds(q0, tq), 0:A_WIDTH] * Q_SCALE,
                           x_ref[0, r, pl.ds(start, kw), A_WIDTH:2 * A_WIDTH],
                           x_ref[0, r, pl.ds(start, kw), 2 * A_WIDTH:3 * A_WIDTH]))
        for i, (j, h) in enumerate(pairs):
            _, _, valid, dist, q, k, _ = blocks[j]
            s_ref[i] = _head_scores(q[:, head_cols(h)], k[:, head_cols(h)], valid, dist, slopes[h])
        ms = [jnp.max(s_ref[i], axis=-1, keepdims=True) for i in range(len(pairs))]
        ls = []
        for i in range(len(pairs)):
            p = jnp.exp(s_ref[i] - ms[i])
            ls.append(jnp.sum(p, axis=-1, keepdims=True))
            p_ref[i] = p.astype(BF16)
        for j in range(nb):
            r, q0, v = blocks[j][0], blocks[j][1], blocks[j][6]
            outs = []
            lse_tile = jnp.zeros((tq, LANES), F32)
            for h in range(A_HEADS):
                i = j * A_HEADS + h
                outs.append(_dot(p_ref[i], v[:, head_cols(h)]) / ls[i])
                lse_tile = jnp.where(lane == h, ms[i] + jnp.log(ls[i]), lse_tile)
            o_ref[0, r, pl.ds(q0, tq), :] = jnp.concatenate(outs, axis=-1).astype(BF16)
            lse_ref[0, r, pl.ds(q0, tq), :] = lse_tile
        return carry

    lax.fori_loop(0, res * per_res // nb, step, 0)


def _attn_a(pa, gi):
    window, dil = A_GROUPS[gi]
    assert window // (2 * dil) == A_HALF_WINDOW
    bsz, _, n, gw = pa.shape
    tq, kw, res, nb = _attn_a_shapes(n, dil)
    assert dil % res == 0 and (res * (n // tq)) % nb == 0
    slopes = _alibi_slopes(len(A_GROUPS) * A_HEADS).reshape(len(A_GROUPS), A_HEADS)[gi]
    slopes = tuple(float(s * np.float32(dil)) for s in slopes)
    block = lambda w: pl.BlockSpec((1, res, n, w), lambda b, r: (b, r, 0, 0))
    return pl.pallas_call(
        functools.partial(_attn_a_kernel, n=n, dil=dil, slopes=slopes),
        grid=(bsz, dil // res),
        in_specs=[block(gw)],
        out_specs=[block(A_WIDTH), block(LANES)],
        out_shape=[jax.ShapeDtypeStruct((bsz, dil, n, A_WIDTH), BF16),
                   jax.ShapeDtypeStruct((bsz, dil, n, LANES), F32)],
        scratch_shapes=[pltpu.VMEM((nb * A_HEADS, tq, kw), F32), pltpu.VMEM((nb * A_HEADS, tq, kw), BF16)],
        compiler_params=_params("parallel", "parallel"),
        name=f"attn_a{gi}",
    )(pa)


def _attn_b_kernel(sink_ref, q_ref, kv_ref, o_ref, s_ref, p_ref, *, n, slopes):
    hw = B_HALF_WINDOW
    tq = min(ATTN_Q_TILE, n)
    kw = min(n, tq + 2 * hw)
    heads = range(B_Q_HEADS)

    def qblock(qi, carry):
        q0 = pl.multiple_of(qi * tq, tq)
        start, valid, dist = _band(q0, n, tq, kw, hw)
        q = q_ref[0, pl.ds(q0, tq), :] * Q_SCALE
        kv = kv_ref[0, pl.ds(start, kw), :]
        for hq in heads:
            hk = hq // B_GROUP
            kh = kv[:, hk * HEAD_DIM:(hk + 1) * HEAD_DIM]
            s_ref[hq] = _head_scores(q[:, hq * HEAD_DIM:(hq + 1) * HEAD_DIM], kh, valid, dist, slopes[hq])
        ms = [jnp.maximum(jnp.max(s_ref[hq], axis=-1, keepdims=True), sink_ref[hq]) for hq in heads]
        ls = []
        for hq in heads:
            p = jnp.exp(s_ref[hq] - ms[hq])
            ls.append(jnp.sum(p, axis=-1, keepdims=True) + jnp.exp(sink_ref[hq] - ms[hq]))
            p_ref[hq] = p.astype(BF16)
        outs = []
        for hq in heads:
            hk = hq // B_GROUP
            vh = kv[:, B_KV_WIDTH + hk * HEAD_DIM:B_KV_WIDTH + (hk + 1) * HEAD_DIM]
            outs.append(_dot(p_ref[hq], vh) / ls[hq])
        o_ref[0, pl.ds(q0, tq), :] = jnp.concatenate(outs, axis=-1).astype(BF16)
        return carry

    lax.fori_loop(0, n // tq, qblock, 0)


def _attn_b(qb, kvb, sink, bsz, seq):
    slopes = tuple(float(s) for s in _alibi_slopes(B_Q_HEADS))
    tq = min(ATTN_Q_TILE, seq)
    kw = min(seq, tq + 2 * B_HALF_WINDOW)
    out = pl.pallas_call(
        functools.partial(_attn_b_kernel, n=seq, slopes=slopes),
        grid=(bsz,),
        in_specs=[pl.BlockSpec(memory_space=pltpu.SMEM),
                  pl.BlockSpec((1, seq, B_Q_WIDTH), lambda b: (b, 0, 0)),
                  pl.BlockSpec((1, seq, 2 * B_KV_WIDTH), lambda b: (b, 0, 0))],
        out_specs=pl.BlockSpec((1, seq, B_Q_WIDTH), lambda b: (b, 0, 0)),
        out_shape=jax.ShapeDtypeStruct((bsz, seq, B_Q_WIDTH), BF16),
        scratch_shapes=[pltpu.VMEM((B_Q_HEADS, tq, kw), F32), pltpu.VMEM((B_Q_HEADS, tq, kw), BF16)],
        compiler_params=_params("parallel"),
        name="attn_b",
    )(sink, qb.reshape(bsz, seq, B_Q_WIDTH), kvb.reshape(bsz, seq, 2 * B_KV_WIDTH))
    return out.reshape(bsz * seq, B_Q_WIDTH)


def _split_bf16(a):
    hi = a.astype(BF16)
    return hi, (a - hi.astype(F32)).astype(BF16)


def _top2(h2, wr_hi, wr_lo):
    h_hi, h_lo = _split_bf16(h2)
    logits = _dot(h_hi, wr_hi) + (_dot(h_hi, wr_lo) + _dot(h_lo, wr_hi))
    lane = lax.broadcasted_iota(jnp.int32, logits.shape, 1)
    l1 = jnp.where(lane < N_EXPERTS, logits, -jnp.inf)
    m1 = jnp.max(l1, axis=-1, keepdims=True)
    i1 = jnp.min(jnp.where(l1 == m1, lane, LANES), axis=-1, keepdims=True)
    l2 = jnp.where(lane == i1, -jnp.inf, l1)
    m2 = jnp.max(l2, axis=-1, keepdims=True)
    i2 = jnp.min(jnp.where(l2 == m2, lane, LANES), axis=-1, keepdims=True)
    e2 = jnp.exp(m2 - m1)
    den = 1.0 + e2
    return jnp.where(lane == 0, i1.astype(F32),
                     jnp.where(lane == 1, i2.astype(F32),
                               jnp.where(lane == 2, 1.0 / den, jnp.where(lane == 3, e2 / den, 0.0))))


def _merge_kernel(*refs, moe):
    ng = len(A_GROUPS)
    x_ref = refs[0]
    o_refs, l_refs = refs[1:1 + ng], refs[1 + ng:1 + 2 * ng]
    yb_ref, gmix_ref, wgate_ref, bgate_ref, wpa_ref, wpb_ref, wout_ref, gffn_ref = refs[1 + 2 * ng:9 + 2 * ng]
    rest = refs[9 + 2 * ng:]
    if moe:
        wr_hi_ref, wr_lo_ref, xo_ref, h2_ref, rout_ref, so_ref, sl_ref = rest
    else:
        xo_ref, h2_ref, so_ref, sl_ref = rest
    x = x_ref[...]
    h = _rms(x, gmix_ref[...]).astype(BF16)
    outs, lses = [], []
    for gi, (_, dil) in enumerate(A_GROUPS):
        if dil == 1:
            outs.append(o_refs[gi][0, 0].astype(F32))
            lses.append(l_refs[gi][0, 0])
        else:
            rows = TOKEN_TILE // dil
            chunks = A_WIDTH // LANES
            for r in range(dil):
                o_r = o_refs[gi][0, r].astype(F32)
                for c in range(chunks):
                    so_ref[gi, c, pl.ds(r, rows, stride=dil), :] = o_r[:, c * LANES:(c + 1) * LANES]
                sl_ref[gi, pl.ds(r, rows, stride=dil), :] = l_refs[gi][0, r]
            outs.append(jnp.concatenate([so_ref[gi, c] for c in range(chunks)], axis=-1))
            lses.append(sl_ref[gi])
    mx = jnp.maximum(jnp.maximum(lses[0], lses[1]), lses[2])
    es = [jnp.exp(l - mx) for l in lses]
    den = es[0] + es[1] + es[2]
    heads = []
    for hh in range(A_HEADS):
        cols = slice(hh * HEAD_DIM, (hh + 1) * HEAD_DIM)
        acc = None
        for g in range(ng):
            term = (es[g][:, hh:hh + 1] / den[:, hh:hh + 1]) * outs[g][:, cols]
            acc = term if acc is None else acc + term
        heads.append(acc)
    y_a = jnp.concatenate(heads, axis=-1).astype(BF16)
    g_a = jax.nn.sigmoid(_dot(h, wgate_ref[:, :D_MODEL]) + bgate_ref[:, :D_MODEL])
    merged = g_a * _dot(y_a, wpa_ref[...])
    g_b = jax.nn.sigmoid(_dot(h, wgate_ref[:, D_MODEL:]) + bgate_ref[:, D_MODEL:])
    merged = merged + g_b * _dot(yb_ref[...], wpb_ref[...])
    xn = x + _dot(merged.astype(BF16), wout_ref[...])
    xo_ref[...] = xn
    h2 = _rms(xn, gffn_ref[...])
    h2_ref[...] = h2.astype(h2_ref.dtype)
    if moe:
        rout_ref[...] = _top2(h2, wr_hi_ref[...], wr_lo_ref[...])


def _merge(x, outs, lses, y_b, seq, gmix, w_gate, b_gate, w_pa, w_pb, w_out, gffn, w_router=None):
    t = x.shape[0]
    tm = TOKEN_TILE
    moe = w_router is not None
    row = lambda i: (i, 0)
    fixed = lambda i: (0, 0)
    full = lambda a: pl.BlockSpec(a.shape, fixed)
    weights = [gmix, w_gate, b_gate, w_pa, w_pb, w_out, gffn]
    out_specs = [pl.BlockSpec((tm, D_MODEL), row), pl.BlockSpec((tm, D_MODEL), row)]
    out_shape = [jax.ShapeDtypeStruct((t, D_MODEL), F32), jax.ShapeDtypeStruct((t, D_MODEL), F32 if moe else BF16)]
    if moe:
        weights.extend(_split_bf16(jnp.zeros((D_MODEL, LANES), F32).at[:, :N_EXPERTS].set(w_router)))
        out_specs.append(pl.BlockSpec((tm, LANES), row))
        out_shape.append(jax.ShapeDtypeStruct((t, LANES), F32))
    ng = len(A_GROUPS)
    return pl.pallas_call(
        functools.partial(_merge_kernel, moe=moe),
        grid=(t // tm,),
        in_specs=[pl.BlockSpec((tm, D_MODEL), row)]
                 + [_residue_spec(seq, dil, A_WIDTH) for _, dil in A_GROUPS]
                 + [_residue_spec(seq, dil, LANES) for _, dil in A_GROUPS]
                 + [pl.BlockSpec((tm, B_Q_WIDTH), row)] + [full(a) for a in weights],
        out_specs=out_specs,
        out_shape=out_shape,
        scratch_shapes=[pltpu.VMEM((ng, A_WIDTH // LANES, tm, LANES), F32), pltpu.VMEM((ng, tm, LANES), F32)],
        compiler_params=_params("parallel"),
        name="merge",
    )(x, *outs, *lses, y_b, *weights)


def _swiglu_step(xb, wg, wu, wd):
    a = _dot(xb, wg)
    return _dot((jax.nn.silu(a) * _dot(xb, wu)).astype(BF16), wd)


def _ffn_kernel(h_ref, x_ref, wg_ref, wu_ref, wd_ref, o_ref, acc_ref):
    j = pl.program_id(1)
    part = _swiglu_step(h_ref[...], wg_ref[...], wu_ref[...], wd_ref[...])

    @pl.when(j == 0)
    def _():
        acc_ref[...] = x_ref[...] + part

    @pl.when(j > 0)
    def _():
        acc_ref[...] += part

    @pl.when(j == pl.num_programs(1) - 1)
    def _():
        o_ref[...] = acc_ref[...]


def _ffn(h2, x, w_g, w_u, w_d):
    t = x.shape[0]
    tm = TOKEN_TILE
    d_ff = w_g.shape[1]
    tf = d_ff // FF_SPLIT_DENSE
    assert tf * FF_SPLIT_DENSE == d_ff and tf % LANES == 0
    row = lambda i, j: (i, 0)
    return pl.pallas_call(
        _ffn_kernel,
        grid=(t // tm, FF_SPLIT_DENSE),
        in_specs=[pl.BlockSpec((tm, D_MODEL), row), pl.BlockSpec((tm, D_MODEL), row),
                  pl.BlockSpec((D_MODEL, tf), lambda i, j: (0, j)), pl.BlockSpec((D_MODEL, tf), lambda i, j: (0, j)),
                  pl.BlockSpec((tf, D_MODEL), lambda i, j: (j, 0))],
        out_specs=pl.BlockSpec((tm, D_MODEL), row),
        out_shape=jax.ShapeDtypeStruct((t, D_MODEL), F32),
        scratch_shapes=[pltpu.VMEM((tm, D_MODEL), F32)],
        compiler_params=_params("parallel", "arbitrary"),
        name="ffn_dense",
    )(h2, x, w_g, w_u, w_d)


def _dispatch_kernel(dest_ref, h_ref, xs_in_ref, xs_ref, sem):
    del xs_in_ref
    tm = h_ref.shape[0]

    def body(r, c):
        for k in range(TOP_K):
            d = dest_ref[0, 0, TOP_K * r + k]
            pltpu.make_async_copy(h_ref.at[pl.ds(r, 1), :], xs_ref.at[pl.ds(d, 1), :], sem).start()
        return c

    lax.fori_loop(0, tm, body, 0)
    for k in range(TOP_K):
        pltpu.make_async_copy(h_ref, xs_ref.at[pl.ds(0, tm), :], sem).wait()


def _dispatch(h2, dest, cap):
    t = h2.shape[0]
    tm = DISPATCH_TILE
    return pl.pallas_call(
        _dispatch_kernel,
        grid=(t // tm,),
        in_specs=[pl.BlockSpec((1, 1, TOP_K * tm), lambda i: (i, 0, 0), memory_space=pltpu.SMEM),
                  pl.BlockSpec((tm, D_MODEL), lambda i: (i, 0)),
                  pl.BlockSpec(memory_space=pl.ANY)],
        out_specs=pl.BlockSpec(memory_space=pl.ANY),
        out_shape=jax.ShapeDtypeStruct((cap, D_MODEL), F32),
        scratch_shapes=[pltpu.SemaphoreType.DMA(())],
        input_output_aliases={2: 0},
        compiler_params=_params("arbitrary"),
        name="dispatch",
    )(dest.reshape(t // tm, 1, TOP_K * tm), h2, jnp.zeros((cap, D_MODEL), F32))


def _expert_kernel(be_ref, xs_ref, wg_ref, wu_ref, wd_ref, ys_ref, xb_ref, acc_ref):
    del be_ref
    j = pl.program_id(1)

    @pl.when(j == 0)
    def _():
        xb_ref[...] = xs_ref[...].astype(BF16)

    part = _swiglu_step(xb_ref[...], wg_ref[0], wu_ref[0], wd_ref[0])

    @pl.when(j == 0)
    def _():
        acc_ref[...] = part

    @pl.when(j > 0)
    def _():
        acc_ref[...] += part

    @pl.when(j == pl.num_programs(1) - 1)
    def _():
        ys_ref[...] = acc_ref[...]


def _experts(xs, block_e, w_g, w_u, w_d):
    cap = xs.shape[0]
    tm = MOE_TILE
    d_ff = w_g.shape[2]
    tf = d_ff // FF_SPLIT_EXPERT
    assert tf * FF_SPLIT_EXPERT == d_ff and tf % LANES == 0
    return pl.pallas_call(
        _expert_kernel,
        grid_spec=pltpu.PrefetchScalarGridSpec(
            num_scalar_prefetch=1,
            grid=(cap // tm, FF_SPLIT_EXPERT),
            in_specs=[pl.BlockSpec((tm, D_MODEL), lambda i, j, be: (i, 0)),
                      pl.BlockSpec((1, D_MODEL, tf), lambda i, j, be: (be[i], 0, j)),
                      pl.BlockSpec((1, D_MODEL, tf), lambda i, j, be: (be[i], 0, j)),
                      pl.BlockSpec((1, tf, D_MODEL), lambda i, j, be: (be[i], j, 0))],
            out_specs=pl.BlockSpec((tm, D_MODEL), lambda i, j, be: (i, 0)),
            scratch_shapes=[pltpu.VMEM((tm, D_MODEL), BF16), pltpu.VMEM((tm, D_MODEL), F32)],
        ),
        out_shape=jax.ShapeDtypeStruct((cap, D_MODEL), F32),
        compiler_params=_params("parallel", "arbitrary"),
        name="experts",
    )(block_e, xs, w_g, w_u, w_d)


def _combine_kernel(dest_ref, dest_next_ref, r_ref, x_ref, ys_ref, g_ref, o_ref, y_ref, sem):
    tm = x_ref.shape[0]
    i = pl.program_id(0)

    def fetch(idx_ref, slot):
        def body(r, c):
            for k in range(TOP_K):
                d = idx_ref[0, 0, TOP_K * r + k]
                pltpu.make_async_copy(ys_ref.at[pl.ds(d, 1), :], y_ref.at[slot, k, pl.ds(r, 1), :],
                                      sem.at[slot, k]).start()
            return c
        lax.fori_loop(0, tm, body, 0)

    @pl.when(i == 0)
    def _():
        fetch(dest_ref, 0)

    @pl.when(i + 1 < pl.num_programs(0))
    def _():
        fetch(dest_next_ref, (i + 1) % 2)

    slot = i % 2
    for k in range(TOP_K):
        pltpu.make_async_copy(ys_ref.at[pl.ds(0, tm), :], y_ref.at[slot, k], sem.at[slot, k]).wait()
    gates = r_ref[...]
    moe = gates[:, 2:3] * y_ref[slot, 0] + gates[:, 3:4] * y_ref[slot, 1]
    o_ref[...] = _rms(x_ref[...] + moe, g_ref[...])


def _combine(ys, dest, rout, x, g_final):
    t = x.shape[0]
    tm = COMBINE_TILE
    steps = t // tm
    dest3 = dest.reshape(steps, 1, TOP_K * tm)
    return pl.pallas_call(
        _combine_kernel,
        grid=(steps,),
        in_specs=[pl.BlockSpec((1, 1, TOP_K * tm), lambda i: (i, 0, 0), memory_space=pltpu.SMEM),
                  pl.BlockSpec((1, 1, TOP_K * tm), lambda i: (jnp.minimum(i + 1, steps - 1), 0, 0),
                               memory_space=pltpu.SMEM),
                  pl.BlockSpec((tm, LANES), lambda i: (i, 0)),
                  pl.BlockSpec((tm, D_MODEL), lambda i: (i, 0)),
                  pl.BlockSpec(memory_space=pl.ANY),
                  pl.BlockSpec((1, D_MODEL), lambda i: (0, 0))],
        out_specs=pl.BlockSpec((tm, D_MODEL), lambda i: (i, 0)),
        out_shape=jax.ShapeDtypeStruct((t, D_MODEL), F32),
        scratch_shapes=[pltpu.VMEM((2, TOP_K, tm, D_MODEL), F32), pltpu.SemaphoreType.DMA((2, TOP_K))],
        compiler_params=_params("arbitrary"),
        name="combine",
    )(dest3, dest3, rout, x, ys, g_final)


def _routing_tables(rout, t):
    tm = MOE_TILE
    e_flat = rout[:, :TOP_K].astype(jnp.int32).reshape(-1)
    onehot = (e_flat[:, None] == jnp.arange(N_EXPERTS, dtype=jnp.int32)[None, :]).astype(jnp.int32)
    csum = jnp.cumsum(onehot, axis=0)
    counts = csum[-1]
    padded = (counts + tm - 1) // tm * tm
    pend = jnp.cumsum(padded)
    pstart = pend - padded
    dest = jnp.sum(onehot * (csum - 1 + pstart[None, :]), axis=1).astype(jnp.int32)
    n_blocks = -(-(t * TOP_K) // tm) + N_EXPERTS
    block_start = jnp.arange(n_blocks, dtype=jnp.int32) * tm
    block_e = jnp.minimum(jnp.sum((pend[None, :] <= block_start[:, None]).astype(jnp.int32), axis=1), N_EXPERTS - 1)
    return dest, block_e, n_blocks * tm


def _trunk(x3, wts):
    bsz, seq, _ = x3.shape
    assert seq % TOKEN_TILE == 0
    x = x3.reshape(bsz * seq, D_MODEL)
    depth = len(wts["layers"])
    for li, lw in enumerate(wts["layers"]):
        *pas, qb, kvb = _proj(x, lw["norm_mix_g"], lw["w_in"], bsz, seq)
        outs, lses = zip(*[_attn_a(pa, gi) for gi, pa in enumerate(pas)])
        y_b = _attn_b(qb, kvb, lw["sink"], bsz, seq)
        mixer = (lw["norm_mix_g"], lw["w_gate"], lw["b_gate"], lw["w_proj_a"], lw["w_proj_b"], lw["w_out"],
                 lw["norm_ffn_g"])
        if "w_router" not in lw:
            assert li < depth - 1
            x, h2 = _merge(x, outs, lses, y_b, seq, *mixer)
            x = _ffn(h2, x, lw["w_ff_gate"], lw["w_ff_up"], lw["w_ff_down"])
        else:
            assert li == depth - 1
            x, h2, rout = _merge(x, outs, lses, y_b, seq, *mixer, w_router=lw["w_router"])
            dest, block_e, cap = _routing_tables(rout, bsz * seq)
            xs = _dispatch(h2, dest, cap)
            ys = _experts(xs, block_e, lw["w_e_gate"], lw["w_e_up"], lw["w_e_down"])
            x = _combine(ys, dest, rout, x, wts["norm_final_g"])
    return x.reshape(bsz, seq, D_MODEL)


def kernel(x_prompt, x_sample, norm_mix_g, w_in, w_gate, b_gate, w_proj_a, w_proj_b, w_out, sink, norm_ffn_g,
           w_ff_gate, w_ff_up, w_ff_down, w_router, w_e_gate, w_e_up, w_e_down, norm_final_g):
    depth = w_in.shape[0]
    layers = []
    for li in range(depth):
        lw = {
            "norm_mix_g": norm_mix_g[li].reshape(1, D_MODEL), "w_in": w_in[li].astype(BF16),
            "w_gate": w_gate[li].astype(BF16), "b_gate": b_gate[li].reshape(1, 2 * D_MODEL),
            "w_proj_a": w_proj_a[li].astype(BF16), "w_proj_b": w_proj_b[li].astype(BF16),
            "w_out": w_out[li].astype(BF16), "sink": sink[li], "norm_ffn_g": norm_ffn_g[li].reshape(1, D_MODEL),
        }
        i = li // 2
        if li % 2 == 0:
            lw.update(w_ff_gate=w_ff_gate[i].astype(BF16), w_ff_up=w_ff_up[i].astype(BF16),
                      w_ff_down=w_ff_down[i].astype(BF16))
        else:
            lw.update(w_router=w_router[i], w_e_gate=w_e_gate[i].astype(BF16), w_e_up=w_e_up[i].astype(BF16),
                      w_e_down=w_e_down[i].astype(BF16))
        layers.append(lw)
    wts = {"layers": layers, "norm_final_g": norm_final_g.reshape(1, D_MODEL)}
    return _trunk(x_prompt, wts), _trunk(x_sample, wts)
```

```python
import functools

import numpy as np
import jax
import jax.numpy as jnp
from jax import lax
from jax.experimental import pallas as pl
from jax.experimental.pallas import tpu as pltpu

F32 = jnp.float32
BF16 = jnp.bfloat16

D_MODEL = 1024
HEAD_DIM = 64
A_GROUPS = ((128, 1), (512, 4), (2048, 16))
A_HALF_WINDOW = 64
A_HEADS = 4
A_WIDTH = A_HEADS * HEAD_DIM
A_COLS = len(A_GROUPS) * 3 * A_WIDTH
B_Q_HEADS = 8
B_KV_HEADS = 2
B_GROUP = B_Q_HEADS // B_KV_HEADS
B_Q_WIDTH = B_Q_HEADS * HEAD_DIM
B_KV_WIDTH = B_KV_HEADS * HEAD_DIM
B_HALF_WINDOW = 128
N_IN = A_COLS + B_Q_WIDTH + 2 * B_KV_WIDTH
N_EXPERTS = 8
TOP_K = 2
RMS_EPS = 1e-6
NEG_INF = -1e30

LANES = 128
TOKEN_TILE = 512
ATTN_Q_TILE = 128
ATTN_A_ROWS = 2048
ATTN_A_BLOCKS = 4
FF_CHUNK_DENSE = 1536
FF_SPLIT_EXPERT = 2
MOE_TILE = 512
DISPATCH_TILE = 1024
COMBINE_TILE = 512
ROW_STRIDE = 4
ROW_LOOP_UNROLL = 8
VMEM_LIMIT = 52 * 1024 * 1024


def _alibi_slopes(n):
    return np.asarray(2.0 ** (-8.0 * np.arange(1, n + 1) / n), dtype=np.float32)


def _params(*sem):
    return pltpu.CompilerParams(dimension_semantics=sem, vmem_limit_bytes=VMEM_LIMIT)


def _rms(x, g):
    return x * lax.rsqrt(jnp.mean(x * x, axis=-1, keepdims=True) + RMS_EPS) * g


def _dot(a, b):
    return jnp.dot(a, b, preferred_element_type=F32)


def _residue_spec(seq, dil, width):
    tiles_per_seq = seq // TOKEN_TILE
    return pl.BlockSpec((1, dil, TOKEN_TILE // dil, width),
                        lambda i: (i // tiles_per_seq, 0, i % tiles_per_seq, 0))


def _proj_kernel(x_ref, g_ref, w_ref, *refs):
    pa_refs, (qb_ref, kv_ref, scr_ref, tmp_ref) = refs[:len(A_GROUPS)], refs[len(A_GROUPS):]
    h = _rms(x_ref[...], g_ref[...]).astype(BF16)
    for gi, (_, dil) in enumerate(A_GROUPS):
        cols = slice(gi * 3 * A_WIDTH, (gi + 1) * 3 * A_WIDTH)
        res = _dot(h, w_ref[:, cols])
        if dil == 1:
            pa_refs[gi][0, 0] = res.astype(BF16)
        else:
            chunks = res.shape[1] // LANES
            for c in range(chunks):
                scr_ref[c] = res[:, c * LANES:(c + 1) * LANES]
            rows = TOKEN_TILE // dil
            if dil <= ROW_STRIDE:
                src = lambda c, r: scr_ref[c, pl.ds(r, rows, stride=dil), :]
            else:
                outer = dil // ROW_STRIDE
                for c in range(chunks):
                    for r1 in range(ROW_STRIDE):
                        tmp_ref[c, r1] = scr_ref[c, pl.ds(r1, TOKEN_TILE // ROW_STRIDE, stride=ROW_STRIDE), :]
                src = lambda c, r: tmp_ref[c, r % ROW_STRIDE, pl.ds(r // ROW_STRIDE, rows, stride=outer), :]
            for r in range(dil):
                parts = [src(c, r) for c in range(chunks)]
                pa_refs[gi][0, r] = jnp.concatenate(parts, axis=-1).astype(BF16)
    qb_ref[...] = _dot(h, w_ref[:, A_COLS:A_COLS + B_Q_WIDTH]).astype(BF16)
    kv_ref[...] = _dot(h, w_ref[:, A_COLS + B_Q_WIDTH:]).astype(BF16)


def _proj(x, g, w_in, bsz, seq):
    t = x.shape[0]
    tm = TOKEN_TILE
    row = lambda i: (i, 0)
    fixed = lambda i: (0, 0)
    gw = 3 * A_WIDTH
    return pl.pallas_call(
        _proj_kernel,
        grid=(t // tm,),
        in_specs=[pl.BlockSpec((tm, D_MODEL), row), pl.BlockSpec((1, D_MODEL), fixed),
                  pl.BlockSpec((D_MODEL, N_IN), fixed)],
        out_specs=[_residue_spec(seq, dil, gw) for _, dil in A_GROUPS]
                  + [pl.BlockSpec((tm, B_Q_WIDTH), row), pl.BlockSpec((tm, 2 * B_KV_WIDTH), row)],
        out_shape=[jax.ShapeDtypeStruct((bsz, dil, seq // dil, gw), BF16) for _, dil in A_GROUPS]
                  + [jax.ShapeDtypeStruct((t, B_Q_WIDTH), BF16), jax.ShapeDtypeStruct((t, 2 * B_KV_WIDTH), BF16)],
        scratch_shapes=[pltpu.VMEM((gw // LANES, tm, LANES), F32),
                        pltpu.VMEM((gw // LANES, ROW_STRIDE, tm // ROW_STRIDE, LANES), F32)],
        compiler_params=_params("parallel"),
        name="proj",
    )(x, g, w_in)


def _band(q0, n, tq, kw, half_window):
    start = pl.multiple_of(jnp.clip(q0 - half_window, 0, n - kw), half_window)
    rel = (lax.broadcasted_iota(jnp.int32, (tq, kw), 0) - lax.broadcasted_iota(jnp.int32, (tq, kw), 1)
           + (q0 - start))
    dist = jnp.abs(rel)
    return start, dist <= half_window, dist.astype(F32)


Q_SCALE = HEAD_DIM ** -0.5


def _head_scores(qh, kh, valid, dist, slope):
    s = lax.dot_general(qh, kh, (((1,), (1,)), ((), ())), preferred_element_type=F32)
    return jnp.where(valid, s - slope * dist, NEG_INF)


def _attn_a_shapes(n, dil):
    tq = min(ATTN_Q_TILE, n)
    res = max(1, min(dil, ATTN_A_ROWS // n))
    return tq, min(n, tq + 2 * A_HALF_WINDOW), res, min(ATTN_A_BLOCKS, res * (n // tq))


def _attn_a_kernel(x_ref, o_ref, lse_ref, s_ref, p_ref, *, n, dil, slopes):
    tq, kw, res, nb = _attn_a_shapes(n, dil)
    per_res = n // tq
    lane = lax.broadcasted_iota(jnp.int32, (tq, LANES), 1)
    pairs = [(j, h) for j in range(nb) for h in range(A_HEADS)]
    head_cols = lambda h: slice(h * HEAD_DIM, (h + 1) * HEAD_DIM)

    def step(it, carry):
        blocks = []
        for j in range(nb):
            blk = it * nb + j
            r = blk // per_res
            q0 = pl.multiple_of((blk % per_res) * tq, tq)
            start, valid, dist = _band(q0, n, tq, kw, A_HALF_WINDOW)
            blocks.append((r, q0, valid, dist, x_ref[0, r, pl.ds(q0, tq), 0:A_WIDTH] * Q_SCALE,
                           x_ref[0, r, pl.ds(start, kw), A_WIDTH:2 * A_WIDTH],
                           x_ref[0, r, pl.ds(start, kw), 2 * A_WIDTH:3 * A_WIDTH]))
        for i, (j, h) in enumerate(pairs):
            _, _, valid, dist, q, k, _ = blocks[j]
            s_ref[i] = _head_scores(q[:, head_cols(h)], k[:, head_cols(h)], valid, dist, slopes[h])
        ms = [jnp.max(s_ref[i], axis=-1, keepdims=True) for i in range(len(pairs))]
        ls = []
        for i in range(len(pairs)):
            p = jnp.exp(s_ref[i] - ms[i])
            ls.append(jnp.sum(p, axis=-1, keepdims=True))
            p_ref[i] = p.astype(BF16)
        for j in range(nb):
            r, q0, v = blocks[j][0], blocks[j][1], blocks[j][6]
            outs = []
            lse_tile = jnp.zeros((tq, LANES), F32)
            for h in range(A_HEADS):
                i = j * A_HEADS + h
                outs.append(_dot(p_ref[i], v[:, head_cols(h)]) / ls[i])
                lse_tile = jnp.where(lane == h, ms[i] + jnp.log(ls[i]), lse_tile)
            o_ref[0, r, pl.ds(q0, tq), :] = jnp.concatenate(outs, axis=-1).astype(BF16)
            lse_ref[0, r, pl.ds(q0, tq), :] = lse_tile
        return carry

    lax.fori_loop(0, res * per_res // nb, step, 0)


def _attn_a(pa, gi):
    window, dil = A_GROUPS[gi]
    assert window // (2 * dil) == A_HALF_WINDOW
    bsz, _, n, gw = pa.shape
    tq, kw, res, nb = _attn_a_shapes(n, dil)
    assert dil % res == 0 and (res * (n // tq)) % nb == 0
    slopes = _alibi_slopes(len(A_GROUPS) * A_HEADS).reshape(len(A_GROUPS), A_HEADS)[gi]
    slopes = tuple(float(s * np.float32(dil)) for s in slopes)
    block = lambda w: pl.BlockSpec((1, res, n, w), lambda b, r: (b, r, 0, 0))
    return pl.pallas_call(
        functools.partial(_attn_a_kernel, n=n, dil=dil, slopes=slopes),
        grid=(bsz, dil // res),
        in_specs=[block(gw)],
        out_specs=[block(A_WIDTH), block(LANES)],
        out_shape=[jax.ShapeDtypeStruct((bsz, dil, n, A_WIDTH), BF16),
                   jax.ShapeDtypeStruct((bsz, dil, n, LANES), F32)],
        scratch_shapes=[pltpu.VMEM((nb * A_HEADS, tq, kw), F32), pltpu.VMEM((nb * A_HEADS, tq, kw), BF16)],
        compiler_params=_params("parallel", "parallel"),
        name=f"attn_a{gi}",
    )(pa)


def _attn_b_kernel(sink_ref, q_ref, kv_ref, o_ref, s_ref, p_ref, *, n, slopes):
    hw = B_HALF_WINDOW
    tq = min(ATTN_Q_TILE, n)
    kw = min(n, tq + 2 * hw)
    heads = range(B_Q_HEADS)

    def qblock(qi, carry):
        q0 = pl.multiple_of(qi * tq, tq)
        start, valid, dist = _band(q0, n, tq, kw, hw)
        q = q_ref[0, pl.ds(q0, tq), :] * Q_SCALE
        kv = kv_ref[0, pl.ds(start, kw), :]
        for hq in heads:
            hk = hq // B_GROUP
            kh = kv[:, hk * HEAD_DIM:(hk + 1) * HEAD_DIM]
            s_ref[hq] = _head_scores(q[:, hq * HEAD_DIM:(hq + 1) * HEAD_DIM], kh, valid, dist, slopes[hq])
        ms = [jnp.maximum(jnp.max(s_ref[hq], axis=-1, keepdims=True), sink_ref[hq]) for hq in heads]
        ls = []
        for hq in heads:
            p = jnp.exp(s_ref[hq] - ms[hq])
            ls.append(jnp.sum(p, axis=-1, keepdims=True) + jnp.exp(sink_ref[hq] - ms[hq]))
            p_ref[hq] = p.astype(BF16)
        outs = []
        for hq in heads:
            hk = hq // B_GROUP
            vh = kv[:, B_KV_WIDTH + hk * HEAD_DIM:B_KV_WIDTH + (hk + 1) * HEAD_DIM]
            outs.append(_dot(p_ref[hq], vh) / ls[hq])
        o_ref[0, pl.ds(q0, tq), :] = jnp.concatenate(outs, axis=-1).astype(BF16)
        return carry

    lax.fori_loop(0, n // tq, qblock, 0)


def _attn_b(qb, kvb, sink, bsz, seq):
    slopes = tuple(float(s) for s in _alibi_slopes(B_Q_HEADS))
    tq = min(ATTN_Q_TILE, seq)
    kw = min(seq, tq + 2 * B_HALF_WINDOW)
    out = pl.pallas_call(
        functools.partial(_attn_b_kernel, n=seq, slopes=slopes),
        grid=(bsz,),
        in_specs=[pl.BlockSpec(memory_space=pltpu.SMEM),
                  pl.BlockSpec((1, seq, B_Q_WIDTH), lambda b: (b, 0, 0)),
                  pl.BlockSpec((1, seq, 2 * B_KV_WIDTH), lambda b: (b, 0, 0))],
        out_specs=pl.BlockSpec((1, seq, B_Q_WIDTH), lambda b: (b, 0, 0)),
        out_shape=jax.ShapeDtypeStruct((bsz, seq, B_Q_WIDTH), BF16),
        scratch_shapes=[pltpu.VMEM((B_Q_HEADS, tq, kw), F32), pltpu.VMEM((B_Q_HEADS, tq, kw), BF16)],
        compiler_params=_params("parallel"),
        name="attn_b",
    )(sink, qb.reshape(bsz, seq, B_Q_WIDTH), kvb.reshape(bsz, seq, 2 * B_KV_WIDTH))
    return out.reshape(bsz * seq, B_Q_WIDTH)


def _split_bf16(a):
    hi = a.astype(BF16)
    return hi, (a - hi.astype(F32)).astype(BF16)


def _top2(h2, wr_hi, wr_lo):
    h_hi, h_lo = _split_bf16(h2)
    logits = _dot(h_hi, wr_hi) + (_dot(h_hi, wr_lo) + _dot(h_lo, wr_hi))
    lane = lax.broadcasted_iota(jnp.int32, logits.shape, 1)
    l1 = jnp.where(lane < N_EXPERTS, logits, -jnp.inf)
    m1 = jnp.max(l1, axis=-1, keepdims=True)
    i1 = jnp.min(jnp.where(l1 == m1, lane, LANES), axis=-1, keepdims=True)
    l2 = jnp.where(lane == i1, -jnp.inf, l1)
    m2 = jnp.max(l2, axis=-1, keepdims=True)
    i2 = jnp.min(jnp.where(l2 == m2, lane, LANES), axis=-1, keepdims=True)
    e2 = jnp.exp(m2 - m1)
    den = 1.0 + e2
    return jnp.where(lane == 0, i1.astype(F32),
                     jnp.where(lane == 1, i2.astype(F32),
                               jnp.where(lane == 2, 1.0 / den, jnp.where(lane == 3, e2 / den, 0.0))))


def _interleave_rows(dst_ref, tmp_ref, piece, dil):
    rows = TOKEN_TILE // dil
    if dil <= ROW_STRIDE:
        for r in range(dil):
            dst_ref[pl.ds(r, rows, stride=dil), :] = piece(r)
        return
    outer = dil // ROW_STRIDE
    for r in range(dil):
        tmp_ref[r % ROW_STRIDE, pl.ds(r // ROW_STRIDE, rows, stride=outer), :] = piece(r)
    for r1 in range(ROW_STRIDE):
        dst_ref[pl.ds(r1, TOKEN_TILE // ROW_STRIDE, stride=ROW_STRIDE), :] = tmp_ref[r1]


def _merge_kernel(*refs, moe):
    ng = len(A_GROUPS)
    x_ref = refs[0]
    o_refs, l_refs = refs[1:1 + ng], refs[1 + ng:1 + 2 * ng]
    yb_ref, gmix_ref, wgate_ref, bgate_ref, wpa_ref, wpb_ref, wout_ref, gffn_ref = refs[1 + 2 * ng:9 + 2 * ng]
    rest = refs[9 + 2 * ng:]
    if moe:
        wr_hi_ref, wr_lo_ref, xo_ref, h2_ref, rout_ref, so_ref, sl_ref, tmp_ref = rest
    else:
        xo_ref, h2_ref, so_ref, sl_ref, tmp_ref = rest
    x = x_ref[...]
    h = _rms(x, gmix_ref[...]).astype(BF16)
    outs, lses = [], []
    for gi, (_, dil) in enumerate(A_GROUPS):
        if dil == 1:
            outs.append(o_refs[gi][0, 0].astype(F32))
            lses.append(l_refs[gi][0, 0])
        else:
            chunks = A_WIDTH // LANES
            for c in range(chunks):
                piece = lambda r, c=c, gi=gi: o_refs[gi][0, r, :, c * LANES:(c + 1) * LANES].astype(F32)
                _interleave_rows(so_ref.at[gi, c], tmp_ref.at[c], piece, dil)
            _interleave_rows(sl_ref.at[gi], tmp_ref.at[chunks], lambda r, gi=gi: l_refs[gi][0, r], dil)
            outs.append(jnp.concatenate([so_ref[gi, c] for c in range(chunks)], axis=-1))
            lses.append(sl_ref[gi])
    mx = jnp.maximum(jnp.maximum(lses[0], lses[1]), lses[2])
    es = [jnp.exp(l - mx) for l in lses]
    den = es[0] + es[1] + es[2]
    heads = []
    for hh in range(A_HEADS):
        cols = slice(hh * HEAD_DIM, (hh + 1) * HEAD_DIM)
        acc = None
        for g in range(ng):
            term = (es[g][:, hh:hh + 1] / den[:, hh:hh + 1]) * outs[g][:, cols]
            acc = term if acc is None else acc + term
        heads.append(acc)
    y_a = jnp.concatenate(heads, axis=-1).astype(BF16)
    g_a = jax.nn.sigmoid(_dot(h, wgate_ref[:, :D_MODEL]) + bgate_ref[:, :D_MODEL])
    merged = g_a * _dot(y_a, wpa_ref[...])
    g_b = jax.nn.sigmoid(_dot(h, wgate_ref[:, D_MODEL:]) + bgate_ref[:, D_MODEL:])
    merged = merged + g_b * _dot(yb_ref[...], wpb_ref[...])
    xn = x + _dot(merged.astype(BF16), wout_ref[...])
    xo_ref[...] = xn
    h2 = _rms(xn, gffn_ref[...])
    h2_ref[...] = h2.astype(h2_ref.dtype)
    if moe:
        rout_ref[...] = _top2(h2, wr_hi_ref[...], wr_lo_ref[...])


def _merge(x, outs, lses, y_b, seq, gmix, w_gate, b_gate, w_pa, w_pb, w_out, gffn, w_router=None):
    t = x.shape[0]
    tm = TOKEN_TILE
    moe = w_router is not None
    row = lambda i: (i, 0)
    fixed = lambda i: (0, 0)
    full = lambda a: pl.BlockSpec(a.shape, fixed)
    weights = [gmix, w_gate, b_gate, w_pa, w_pb, w_out, gffn]
    out_specs = [pl.BlockSpec((tm, D_MODEL), row), pl.BlockSpec((tm, D_MODEL), row)]
    out_shape = [jax.ShapeDtypeStruct((t, D_MODEL), F32), jax.ShapeDtypeStruct((t, D_MODEL), F32 if moe else BF16)]
    if moe:
        weights.extend(_split_bf16(jnp.zeros((D_MODEL, LANES), F32).at[:, :N_EXPERTS].set(w_router)))
        out_specs.append(pl.BlockSpec((tm, LANES), row))
        out_shape.append(jax.ShapeDtypeStruct((t, LANES), F32))
    ng = len(A_GROUPS)
    return pl.pallas_call(
        functools.partial(_merge_kernel, moe=moe),
        grid=(t // tm,),
        in_specs=[pl.BlockSpec((tm, D_MODEL), row)]
                 + [_residue_spec(seq, dil, A_WIDTH) for _, dil in A_GROUPS]
                 + [_residue_spec(seq, dil, LANES) for _, dil in A_GROUPS]
                 + [pl.BlockSpec((tm, B_Q_WIDTH), row)] + [full(a) for a in weights],
        out_specs=out_specs,
        out_shape=out_shape,
        scratch_shapes=[pltpu.VMEM((ng, A_WIDTH // LANES, tm, LANES), F32), pltpu.VMEM((ng, tm, LANES), F32),
                        pltpu.VMEM((A_WIDTH // LANES + 1, ROW_STRIDE, tm // ROW_STRIDE, LANES), F32)],
        compiler_params=_params("parallel"),
        name="merge",
    )(x, *outs, *lses, y_b, *weights)


def _swiglu_step(xb, wg, wu, wd):
    a = _dot(xb, wg)
    return _dot((jax.nn.silu(a) * _dot(xb, wu)).astype(BF16), wd)


def _ffn_kernel(h_ref, x_ref, wg_ref, wu_ref, wd_ref, o_ref, hm_ref):
    h = h_ref[...]
    d_ff = wg_ref.shape[1]
    for lo in range(0, d_ff, FF_CHUNK_DENSE):
        cols = slice(lo, min(lo + FF_CHUNK_DENSE, d_ff))
        hm_ref[:, cols] = (jax.nn.silu(_dot(h, wg_ref[:, cols])) * _dot(h, wu_ref[:, cols])).astype(BF16)
    o_ref[...] = x_ref[...] + _dot(hm_ref[...], wd_ref[...])


def _ffn(h2, x, w_g, w_u, w_d):
    t = x.shape[0]
    tm = TOKEN_TILE
    d_ff = w_g.shape[1]
    row = lambda i: (i, 0)
    fixed = lambda i: (0, 0)
    return pl.pallas_call(
        _ffn_kernel,
        grid=(t // tm,),
        in_specs=[pl.BlockSpec((tm, D_MODEL), row), pl.BlockSpec((tm, D_MODEL), row),
                  pl.BlockSpec((D_MODEL, d_ff), fixed), pl.BlockSpec((D_MODEL, d_ff), fixed),
                  pl.BlockSpec((d_ff, D_MODEL), fixed)],
        out_specs=pl.BlockSpec((tm, D_MODEL), row),
        out_shape=jax.ShapeDtypeStruct((t, D_MODEL), F32),
        scratch_shapes=[pltpu.VMEM((tm, d_ff), BF16)],
        compiler_params=_params("parallel"),
        name="ffn_dense",
    )(h2, x, w_g, w_u, w_d)


def _dispatch_kernel(zero_ref, dest_ref, h_ref, xs_ref, zbuf_ref, sem, zsem):
    tm = h_ref.shape[0]

    @pl.when(pl.program_id(0) == 0)
    def _():
        zbuf_ref[...] = jnp.zeros_like(zbuf_ref)
        n_zero = zero_ref.shape[0]
        zero_copy = lambda z: pltpu.make_async_copy(
            zbuf_ref, xs_ref.at[pl.ds(pl.multiple_of(zero_ref[z], MOE_TILE), MOE_TILE), :], zsem)
        for z in range(n_zero):
            zero_copy(z).start()
            zero_copy(z).wait()

    def body(r, c):
        for k in range(TOP_K):
            d = dest_ref[0, 0, TOP_K * r + k]
            pltpu.make_async_copy(h_ref.at[pl.ds(r, 1), :], xs_ref.at[pl.ds(d, 1), :], sem).start(priority=k)
        return c

    lax.fori_loop(0, tm, body, 0, unroll=ROW_LOOP_UNROLL)
    for k in range(TOP_K):
        pltpu.make_async_copy(h_ref, xs_ref.at[pl.ds(0, tm), :], sem).wait()


def _dispatch(h2, dest, zero_blocks, cap):
    t = h2.shape[0]
    tm = DISPATCH_TILE
    return pl.pallas_call(
        _dispatch_kernel,
        grid_spec=pltpu.PrefetchScalarGridSpec(
            num_scalar_prefetch=1,
            grid=(t // tm,),
            in_specs=[pl.BlockSpec((1, 1, TOP_K * tm), lambda i, z: (i, 0, 0), memory_space=pltpu.SMEM),
                      pl.BlockSpec((tm, D_MODEL), lambda i, z: (i, 0))],
            out_specs=pl.BlockSpec(memory_space=pl.ANY),
            scratch_shapes=[pltpu.VMEM((MOE_TILE, D_MODEL), F32), pltpu.SemaphoreType.DMA(()),
                            pltpu.SemaphoreType.DMA(())],
        ),
        out_shape=jax.ShapeDtypeStruct((cap, D_MODEL), F32),
        compiler_params=_params("arbitrary"),
        name="dispatch",
    )(zero_blocks, dest.reshape(t // tm, 1, TOP_K * tm), h2)


def _expert_kernel(be_ref, xs_ref, wg_ref, wu_ref, wd_ref, ys_ref, xb_ref, acc_ref):
    del be_ref
    j = pl.program_id(1)

    @pl.when(j == 0)
    def _():
        xb_ref[...] = xs_ref[...].astype(BF16)

    part = _swiglu_step(xb_ref[...], wg_ref[0], wu_ref[0], wd_ref[0])

    @pl.when(j == 0)
    def _():
        acc_ref[...] = part

    @pl.when(j > 0)
    def _():
        acc_ref[...] += part

    @pl.when(j == pl.num_programs(1) - 1)
    def _():
        ys_ref[...] = acc_ref[...]


def _experts(xs, block_e, w_g, w_u, w_d):
    cap = xs.shape[0]
    tm = MOE_TILE
    d_ff = w_g.shape[2]
    tf = d_ff // FF_SPLIT_EXPERT
    assert tf * FF_SPLIT_EXPERT == d_ff and tf % LANES == 0
    return pl.pallas_call(
        _expert_kernel,
        grid_spec=pltpu.PrefetchScalarGridSpec(
            num_scalar_prefetch=1,
            grid=(cap // tm, FF_SPLIT_EXPERT),
            in_specs=[pl.BlockSpec((tm, D_MODEL), lambda i, j, be: (i, 0)),
                      pl.BlockSpec((1, D_MODEL, tf), lambda i, j, be: (be[i], 0, j)),
                      pl.BlockSpec((1, D_MODEL, tf), lambda i, j, be: (be[i], 0, j)),
                      pl.BlockSpec((1, tf, D_MODEL), lambda i, j, be: (be[i], j, 0))],
            out_specs=pl.BlockSpec((tm, D_MODEL), lambda i, j, be: (i, 0)),
            scratch_shapes=[pltpu.VMEM((tm, D_MODEL), BF16), pltpu.VMEM((tm, D_MODEL), F32)],
        ),
        out_shape=jax.ShapeDtypeStruct((cap, D_MODEL), F32),
        compiler_params=_params("parallel", "arbitrary"),
        name="experts",
    )(block_e, xs, w_g, w_u, w_d)


def _combine_kernel(dest_ref, dest_next_ref, r_ref, x_ref, ys_ref, g_ref, o_ref, y_ref, sem):
    tm = x_ref.shape[0]
    i = pl.program_id(0)

    def fetch(idx_ref, slot):
        def body(r, c):
            for k in range(TOP_K):
                d = idx_ref[0, 0, TOP_K * r + k]
                pltpu.make_async_copy(ys_ref.at[pl.ds(d, 1), :], y_ref.at[slot, k, pl.ds(r, 1), :],
                                      sem.at[slot, k]).start(priority=k)
            return c
        lax.fori_loop(0, tm, body, 0, unroll=ROW_LOOP_UNROLL)

    @pl.when(i == 0)
    def _():
        fetch(dest_ref, 0)

    @pl.when(i + 1 < pl.num_programs(0))
    def _():
        fetch(dest_next_ref, (i + 1) % 2)

    slot = i % 2
    for k in range(TOP_K):
        pltpu.make_async_copy(ys_ref.at[pl.ds(0, tm), :], y_ref.at[slot, k], sem.at[slot, k]).wait()
    gates = r_ref[...]
    moe = gates[:, 2:3] * y_ref[slot, 0] + gates[:, 3:4] * y_ref[slot, 1]
    o_ref[...] = _rms(x_ref[...] + moe, g_ref[...])


def _combine(ys, dest, rout, x, g_final):
    t = x.shape[0]
    tm = COMBINE_TILE
    steps = t // tm
    dest3 = dest.reshape(steps, 1, TOP_K * tm)
    return pl.pallas_call(
        _combine_kernel,
        grid=(steps,),
        in_specs=[pl.BlockSpec((1, 1, TOP_K * tm), lambda i: (i, 0, 0), memory_space=pltpu.SMEM),
                  pl.BlockSpec((1, 1, TOP_K * tm), lambda i: (jnp.minimum(i + 1, steps - 1), 0, 0),
                               memory_space=pltpu.SMEM),
                  pl.BlockSpec((tm, LANES), lambda i: (i, 0)),
                  pl.BlockSpec((tm, D_MODEL), lambda i: (i, 0)),
                  pl.BlockSpec(memory_space=pl.ANY),
                  pl.BlockSpec((1, D_MODEL), lambda i: (0, 0))],
        out_specs=pl.BlockSpec((tm, D_MODEL), lambda i: (i, 0)),
        out_shape=jax.ShapeDtypeStruct((t, D_MODEL), F32),
        scratch_shapes=[pltpu.VMEM((2, TOP_K, tm, D_MODEL), F32), pltpu.SemaphoreType.DMA((2, TOP_K))],
        compiler_params=_params("arbitrary"),
        name="combine",
    )(dest3, dest3, rout, x, ys, g_final)


def _routing_tables(rout, t):
    tm = MOE_TILE
    e_flat = rout[:, :TOP_K].astype(jnp.int32).reshape(-1)
    onehot = (e_flat[:, None] == jnp.arange(N_EXPERTS, dtype=jnp.int32)[None, :]).astype(jnp.int32)
    csum = jnp.cumsum(onehot, axis=0)
    counts = csum[-1]
    padded = (counts + tm - 1) // tm * tm
    pend = jnp.cumsum(padded)
    pstart = pend - padded
    dest = jnp.sum(onehot * (csum - 1 + pstart[None, :]), axis=1).astype(jnp.int32)
    n_blocks = -(-(t * TOP_K) // tm) + N_EXPERTS
    block_start = jnp.arange(n_blocks, dtype=jnp.int32) * tm
    block_e = jnp.minimum(jnp.sum((pend[None, :] <= block_start[:, None]).astype(jnp.int32), axis=1), N_EXPERTS - 1)
    zero_blocks = jnp.concatenate([jnp.maximum(pend - tm, 0), block_start[n_blocks - N_EXPERTS:]]).astype(jnp.int32)
    return dest, block_e, zero_blocks, n_blocks * tm


def _trunk(x3, wts):
    bsz, seq, _ = x3.shape
    assert seq % TOKEN_TILE == 0
    x = x3.reshape(bsz * seq, D_MODEL)
    depth = len(wts["layers"])
    for li, lw in enumerate(wts["layers"]):
        *pas, qb, kvb = _proj(x, lw["norm_mix_g"], lw["w_in"], bsz, seq)
        outs, lses = zip(*[_attn_a(pa, gi) for gi, pa in enumerate(pas)])
        y_b = _attn_b(qb, kvb, lw["sink"], bsz, seq)
        mixer = (lw["norm_mix_g"], lw["w_gate"], lw["b_gate"], lw["w_proj_a"], lw["w_proj_b"], lw["w_out"],
                 lw["norm_ffn_g"])
        if "w_router" not in lw:
            assert li < depth - 1
            x, h2 = _merge(x, outs, lses, y_b, seq, *mixer)
            x = _ffn(h2, x, lw["w_ff_gate"], lw["w_ff_up"], lw["w_ff_down"])
        else:
            assert li == depth - 1
            x, h2, rout = _merge(x, outs, lses, y_b, seq, *mixer, w_router=lw["w_router"])
            dest, block_e, zero_blocks, cap = _routing_tables(rout, bsz * seq)
            xs = _dispatch(h2, dest, zero_blocks, cap)
            ys = _experts(xs, block_e, lw["w_e_gate"], lw["w_e_up"], lw["w_e_down"])
            x = _combine(ys, dest, rout, x, wts["norm_final_g"])
    return x.reshape(bsz, seq, D_MODEL)


def kernel(x_prompt, x_sample, norm_mix_g, w_in, w_gate, b_gate, w_proj_a, w_proj_b, w_out, sink, norm_ffn_g,
           w_ff_gate, w_ff_up, w_ff_down, w_router, w_e_gate, w_e_up, w_e_down, norm_final_g):
    depth = w_in.shape[0]
    layers = []
    for li in range(depth):
        lw = {
            "norm_mix_g": norm_mix_g[li].reshape(1, D_MODEL), "w_in": w_in[li].astype(BF16),
            "w_gate": w_gate[li].astype(BF16), "b_gate": b_gate[li].reshape(1, 2 * D_MODEL),
            "w_proj_a": w_proj_a[li].astype(BF16), "w_proj_b": w_proj_b[li].astype(BF16),
            "w_out": w_out[li].astype(BF16), "sink": sink[li], "norm_ffn_g": norm_ffn_g[li].reshape(1, D_MODEL),
        }
        i = li // 2
        if li % 2 == 0:
            lw.update(w_ff_gate=w_ff_gate[i].astype(BF16), w_ff_up=w_ff_up[i].astype(BF16),
                      w_ff_down=w_ff_down[i].astype(BF16))
        else:
            lw.update(w_router=w_router[i], w_e_gate=w_e_gate[i].astype(BF16), w_e_up=w_e_up[i].astype(BF16),
                      w_e_down=w_e_down[i].astype(BF16))
        layers.append(lw)
    wts = {"layers": layers, "norm_final_g": norm_final_g.reshape(1, D_MODEL)}
    return _trunk(x_prompt, wts), _trunk(x_sample, wts)
```

```python
import functools

import numpy as np
import jax
import jax.numpy as jnp
from jax import lax
from jax.experimental import pallas as pl
from jax.experimental.pallas import tpu as pltpu

F32 = jnp.float32
BF16 = jnp.bfloat16

D_MODEL = 1024
HEAD_DIM = 64
A_GROUPS = ((128, 1), (512, 4), (2048, 16))
A_HALF_WINDOW = 64
A_HEADS = 4
A_WIDTH = A_HEADS * HEAD_DIM
A_COLS = len(A_GROUPS) * 3 * A_WIDTH
B_Q_HEADS = 8
B_KV_HEADS = 2
B_GROUP = B_Q_HEADS // B_KV_HEADS
B_Q_WIDTH = B_Q_HEADS * HEAD_DIM
B_KV_WIDTH = B_KV_HEADS * HEAD_DIM
B_HALF_WINDOW = 128
N_IN = A_COLS + B_Q_WIDTH + 2 * B_KV_WIDTH
N_EXPERTS = 8
TOP_K = 2
RMS_EPS = 1e-6
NEG_INF = -1e30

LANES = 128
TOKEN_TILE = 512
ATTN_Q_TILE = 128
ATTN_A_ROWS = 2048
ATTN_A_BLOCKS = 4
FF_CHUNK_DENSE = 1536
FF_SPLIT_EXPERT = 2
MOE_TILE = 512
DISPATCH_TILE = 1024
COMBINE_TILE = 512
ROW_STRIDE = 4
ROW_LOOP_UNROLL = 8
VMEM_LIMIT = 52 * 1024 * 1024


def _alibi_slopes(n):
    return np.asarray(2.0 ** (-8.0 * np.arange(1, n + 1) / n), dtype=np.float32)


def _params(*sem):
    return pltpu.CompilerParams(dimension_semantics=sem, vmem_limit_bytes=VMEM_LIMIT)


def _rms(x, g):
    return x * lax.rsqrt(jnp.mean(x * x, axis=-1, keepdims=True) + RMS_EPS) * g


def _dot(a, b):
    return jnp.dot(a, b, preferred_element_type=F32)


def _residue_spec(seq, dil, width):
    tiles_per_seq = seq // TOKEN_TILE
    return pl.BlockSpec((1, dil, TOKEN_TILE // dil, width),
                        lambda i: (i // tiles_per_seq, 0, i % tiles_per_seq, 0))


def _proj_kernel(x_ref, g_ref, w_ref, *refs):
    pa_refs, (qb_ref, kv_ref, scr_ref, tmp_ref) = refs[:len(A_GROUPS)], refs[len(A_GROUPS):]
    h = _rms(x_ref[...], g_ref[...]).astype(BF16)
    for gi, (_, dil) in enumerate(A_GROUPS):
        cols = slice(gi * 3 * A_WIDTH, (gi + 1) * 3 * A_WIDTH)
        res = _dot(h, w_ref[:, cols])
        if dil == 1:
            pa_refs[gi][0, 0] = res.astype(BF16)
        else:
            chunks = res.shape[1] // LANES
            for c in range(chunks):
                scr_ref[c] = res[:, c * LANES:(c + 1) * LANES]
            rows = TOKEN_TILE // dil
            if dil <= ROW_STRIDE:
                src = lambda c, r: scr_ref[c, pl.ds(r, rows, stride=dil), :]
            else:
                outer = dil // ROW_STRIDE
                for c in range(chunks):
                    for r1 in range(ROW_STRIDE):
                        tmp_ref[c, r1] = scr_ref[c, pl.ds(r1, TOKEN_TILE // ROW_STRIDE, stride=ROW_STRIDE), :]
                src = lambda c, r: tmp_ref[c, r % ROW_STRIDE, pl.ds(r // ROW_STRIDE, rows, stride=outer), :]
            for r in range(dil):
                parts = [src(c, r) for c in range(chunks)]
                pa_refs[gi][0, r] = jnp.concatenate(parts, axis=-1).astype(BF16)
    qb_ref[...] = _dot(h, w_ref[:, A_COLS:A_COLS + B_Q_WIDTH]).astype(BF16)
    kv_ref[...] = _dot(h, w_ref[:, A_COLS + B_Q_WIDTH:]).astype(BF16)


def _proj(x, g, w_in, bsz, seq):
    t = x.shape[0]
    tm = TOKEN_TILE
    row = lambda i: (i, 0)
    fixed = lambda i: (0, 0)
    gw = 3 * A_WIDTH
    return pl.pallas_call(
        _proj_kernel,
        grid=(t // tm,),
        in_specs=[pl.BlockSpec((tm, D_MODEL), row), pl.BlockSpec((1, D_MODEL), fixed),
                  pl.BlockSpec((D_MODEL, N_IN), fixed)],
        out_specs=[_residue_spec(seq, dil, gw) for _, dil in A_GROUPS]
                  + [pl.BlockSpec((tm, B_Q_WIDTH), row), pl.BlockSpec((tm, 2 * B_KV_WIDTH), row)],
        out_shape=[jax.ShapeDtypeStruct((bsz, dil, seq // dil, gw), BF16) for _, dil in A_GROUPS]
                  + [jax.ShapeDtypeStruct((t, B_Q_WIDTH), BF16), jax.ShapeDtypeStruct((t, 2 * B_KV_WIDTH), BF16)],
        scratch_shapes=[pltpu.VMEM((gw // LANES, tm, LANES), F32),
                        pltpu.VMEM((gw // LANES, ROW_STRIDE, tm // ROW_STRIDE, LANES), F32)],
        compiler_params=_params("parallel"),
        name="proj",
    )(x, g, w_in)


BAND_CASES = 3


def _band(q0, n, kw, half_window):
    start = pl.multiple_of(jnp.clip(q0 - half_window, 0, n - kw), half_window)
    return start, (q0 - start) // half_window


def _fill_bias(bias_ref, slopes, tq, kw, half_window):
    rel = lax.broadcasted_iota(jnp.int32, (tq, kw), 0) - lax.broadcasted_iota(jnp.int32, (tq, kw), 1)
    for case in range(BAND_CASES):
        dist = jnp.abs(rel + case * half_window)
        for h, slope in enumerate(slopes):
            bias_ref[case, h] = jnp.where(dist <= half_window, -slope * dist.astype(F32), NEG_INF)


Q_SCALE = HEAD_DIM ** -0.5


def _head_scores(qh, kh, bias):
    return lax.dot_general(qh, kh, (((1,), (1,)), ((), ())), preferred_element_type=F32) + bias


def _attn_a_shapes(n, dil):
    tq = min(ATTN_Q_TILE, n)
    res = max(1, min(dil, ATTN_A_ROWS // n))
    return tq, min(n, tq + 2 * A_HALF_WINDOW), res, min(ATTN_A_BLOCKS, res * (n // tq))


def _attn_a_kernel(x_ref, o_ref, lse_ref, s_ref, p_ref, bias_ref, *, n, dil, slopes):
    tq, kw, res, nb = _attn_a_shapes(n, dil)
    per_res = n // tq
    lane = lax.broadcasted_iota(jnp.int32, (tq, LANES), 1)
    pairs = [(j, h) for j in range(nb) for h in range(A_HEADS)]
    head_cols = lambda h: slice(h * HEAD_DIM, (h + 1) * HEAD_DIM)

    @pl.when((pl.program_id(0) == 0) & (pl.program_id(1) == 0))
    def _():
        _fill_bias(bias_ref, slopes, tq, kw, A_HALF_WINDOW)

    def step(it, carry):
        blocks = []
        for j in range(nb):
            blk = it * nb + j
            r = blk // per_res
            q0 = pl.multiple_of((blk % per_res) * tq, tq)
            start, case = _band(q0, n, kw, A_HALF_WINDOW)
            blocks.append((r, q0, case, x_ref[0, r, pl.ds(q0, tq), 0:A_WIDTH] * Q_SCALE,
                           x_ref[0, r, pl.ds(start, kw), A_WIDTH:2 * A_WIDTH],
                           x_ref[0, r, pl.ds(start, kw), 2 * A_WIDTH:3 * A_WIDTH]))
        for i, (j, h) in enumerate(pairs):
            _, _, case, q, k, _ = blocks[j]
            s_ref[i] = _head_scores(q[:, head_cols(h)], k[:, head_cols(h)], bias_ref[case, h])
        ms = [jnp.max(s_ref[i], axis=-1, keepdims=True) for i in range(len(pairs))]
        ls = []
        for i in range(len(pairs)):
            p = jnp.exp(s_ref[i] - ms[i])
            ls.append(jnp.sum(p, axis=-1, keepdims=True))
            p_ref[i] = p.astype(BF16)
        for j in range(nb):
            r, q0, v = blocks[j][0], blocks[j][1], blocks[j][5]
            outs = []
            lse_tile = jnp.zeros((tq, LANES), F32)
            for h in range(A_HEADS):
                i = j * A_HEADS + h
                outs.append(_dot(p_ref[i], v[:, head_cols(h)]) / ls[i])
                lse_tile = jnp.where(lane == h, ms[i] + jnp.log(ls[i]), lse_tile)
            o_ref[0, r, pl.ds(q0, tq), :] = jnp.concatenate(outs, axis=-1).astype(BF16)
            lse_ref[0, r, pl.ds(q0, tq), :] = lse_tile
        return carry

    lax.fori_loop(0, res * per_res // nb, step, 0)


def _attn_a(pa, gi):
    window, dil = A_GROUPS[gi]
    assert window // (2 * dil) == A_HALF_WINDOW
    bsz, _, n, gw = pa.shape
    tq, kw, res, nb = _attn_a_shapes(n, dil)
    assert dil % res == 0 and (res * (n // tq)) % nb == 0
    slopes = _alibi_slopes(len(A_GROUPS) * A_HEADS).reshape(len(A_GROUPS), A_HEADS)[gi]
    slopes = tuple(float(s * np.float32(dil)) for s in slopes)
    block = lambda w: pl.BlockSpec((1, res, n, w), lambda b, r: (b, r, 0, 0))
    return pl.pallas_call(
        functools.partial(_attn_a_kernel, n=n, dil=dil, slopes=slopes),
        grid=(bsz, dil // res),
        in_specs=[block(gw)],
        out_specs=[block(A_WIDTH), block(LANES)],
        out_shape=[jax.ShapeDtypeStruct((bsz, dil, n, A_WIDTH), BF16),
                   jax.ShapeDtypeStruct((bsz, dil, n, LANES), F32)],
        scratch_shapes=[pltpu.VMEM((nb * A_HEADS, tq, kw), F32), pltpu.VMEM((nb * A_HEADS, tq, kw), BF16),
                        pltpu.VMEM((BAND_CASES, A_HEADS, tq, kw), F32)],
        compiler_params=_params("arbitrary", "arbitrary"),
        name=f"attn_a{gi}",
    )(pa)


def _attn_b_kernel(sink_ref, q_ref, kv_ref, o_ref, s_ref, p_ref, bias_ref, *, n, slopes):
    hw = B_HALF_WINDOW
    tq = min(ATTN_Q_TILE, n)
    kw = min(n, tq + 2 * hw)
    heads = range(B_Q_HEADS)

    @pl.when(pl.program_id(0) == 0)
    def _():
        _fill_bias(bias_ref, slopes, tq, kw, hw)

    def qblock(qi, carry):
        q0 = pl.multiple_of(qi * tq, tq)
        start, case = _band(q0, n, kw, hw)
        q = q_ref[0, pl.ds(q0, tq), :] * Q_SCALE
        kv = kv_ref[0, pl.ds(start, kw), :]
        for hq in heads:
            hk = hq // B_GROUP
            kh = kv[:, hk * HEAD_DIM:(hk + 1) * HEAD_DIM]
            s_ref[hq] = _head_scores(q[:, hq * HEAD_DIM:(hq + 1) * HEAD_DIM], kh, bias_ref[case, hq])
        ms = [jnp.maximum(jnp.max(s_ref[hq], axis=-1, keepdims=True), sink_ref[hq]) for hq in heads]
        ls = []
        for hq in heads:
            p = jnp.exp(s_ref[hq] - ms[hq])
            ls.append(jnp.sum(p, axis=-1, keepdims=True) + jnp.exp(sink_ref[hq] - ms[hq]))
            p_ref[hq] = p.astype(BF16)
        outs = []
        for hq in heads:
            hk = hq // B_GROUP
            vh = kv[:, B_KV_WIDTH + hk * HEAD_DIM:B_KV_WIDTH + (hk + 1) * HEAD_DIM]
            outs.append(_dot(p_ref[hq], vh) / ls[hq])
        o_ref[0, pl.ds(q0, tq), :] = jnp.concatenate(outs, axis=-1).astype(BF16)
        return carry

    lax.fori_loop(0, n // tq, qblock, 0)


def _attn_b(qb, kvb, sink, bsz, seq):
    slopes = tuple(float(s) for s in _alibi_slopes(B_Q_HEADS))
    tq = min(ATTN_Q_TILE, seq)
    kw = min(seq, tq + 2 * B_HALF_WINDOW)
    out = pl.pallas_call(
        functools.partial(_attn_b_kernel, n=seq, slopes=slopes),
        grid=(bsz,),
        in_specs=[pl.BlockSpec(memory_space=pltpu.SMEM),
                  pl.BlockSpec((1, seq, B_Q_WIDTH), lambda b: (b, 0, 0)),
                  pl.BlockSpec((1, seq, 2 * B_KV_WIDTH), lambda b: (b, 0, 0))],
        out_specs=pl.BlockSpec((1, seq, B_Q_WIDTH), lambda b: (b, 0, 0)),
        out_shape=jax.ShapeDtypeStruct((bsz, seq, B_Q_WIDTH), BF16),
        scratch_shapes=[pltpu.VMEM((B_Q_HEADS, tq, kw), F32), pltpu.VMEM((B_Q_HEADS, tq, kw), BF16),
                        pltpu.VMEM((BAND_CASES, B_Q_HEADS, tq, kw), F32)],
        compiler_params=_params("arbitrary"),
        name="attn_b",
    )(sink, qb.reshape(bsz, seq, B_Q_WIDTH), kvb.reshape(bsz, seq, 2 * B_KV_WIDTH))
    return out.reshape(bsz * seq, B_Q_WIDTH)


def _split_bf16(a):
    hi = a.astype(BF16)
    return hi, (a - hi.astype(F32)).astype(BF16)


def _top2(h2, wr_a, wr_b):
    tm = h2.shape[0]
    nt = (((1,), (1,)), ((), ()))
    h_hi, h_lo = _split_bf16(h2)
    by_hi = lax.dot_general(wr_a, h_hi, nt, preferred_element_type=F32)
    by_lo = lax.dot_general(wr_b, h_lo, nt, preferred_element_type=F32)
    logits = by_hi[:N_EXPERTS] + (by_hi[N_EXPERTS:] + by_lo[:N_EXPERTS])
    row = lax.broadcasted_iota(jnp.int32, logits.shape, 0)
    m1 = jnp.max(logits, axis=0, keepdims=True)
    i1 = jnp.min(jnp.where(logits == m1, row, N_EXPERTS), axis=0, keepdims=True)
    l2 = jnp.where(row == i1, -jnp.inf, logits)
    m2 = jnp.max(l2, axis=0, keepdims=True)
    i2 = jnp.min(jnp.where(l2 == m2, row, N_EXPERTS), axis=0, keepdims=True)
    e2 = jnp.exp(m2 - m1)
    den = 1.0 + e2
    packed = jnp.where(row == 0, i1.astype(F32),
                       jnp.where(row == 1, i2.astype(F32),
                                 jnp.where(row == 2, 1.0 / den, jnp.where(row == 3, e2 / den, 0.0))))
    padded = jnp.concatenate([packed, jnp.zeros((LANES - N_EXPERTS, tm), F32)], axis=0)
    return padded.T


def _interleave_rows(dst_ref, tmp_ref, piece, dil):
    rows = TOKEN_TILE // dil
    if dil <= ROW_STRIDE:
        for r in range(dil):
            dst_ref[pl.ds(r, rows, stride=dil), :] = piece(r)
        return
    outer = dil // ROW_STRIDE
    for r in range(dil):
        tmp_ref[r % ROW_STRIDE, pl.ds(r // ROW_STRIDE, rows, stride=outer), :] = piece(r)
    for r1 in range(ROW_STRIDE):
        dst_ref[pl.ds(r1, TOKEN_TILE // ROW_STRIDE, stride=ROW_STRIDE), :] = tmp_ref[r1]


def _merge_kernel(*refs, moe):
    ng = len(A_GROUPS)
    x_ref = refs[0]
    o_refs, l_refs = refs[1:1 + ng], refs[1 + ng:1 + 2 * ng]
    yb_ref, gmix_ref, wgate_ref, bgate_ref, wpa_ref, wpb_ref, wout_ref, gffn_ref = refs[1 + 2 * ng:9 + 2 * ng]
    rest = refs[9 + 2 * ng:]
    if moe:
        wr_hi_ref, wr_lo_ref, xo_ref, h2_ref, rout_ref, so_ref, sl_ref, tmp_ref = rest
    else:
        xo_ref, h2_ref, so_ref, sl_ref, tmp_ref = rest
    x = x_ref[...]
    h = _rms(x, gmix_ref[...]).astype(BF16)
    z_b = _dot(yb_ref[...], wpb_ref[...])
    g_a = jax.nn.sigmoid(_dot(h, wgate_ref[:, :D_MODEL]) + bgate_ref[:, :D_MODEL])
    g_b = jax.nn.sigmoid(_dot(h, wgate_ref[:, D_MODEL:]) + bgate_ref[:, D_MODEL:])
    outs, lses = [], []
    for gi, (_, dil) in enumerate(A_GROUPS):
        if dil == 1:
            outs.append(o_refs[gi][0, 0].astype(F32))
            lses.append(l_refs[gi][0, 0])
        else:
            chunks = A_WIDTH // LANES
            for c in range(chunks):
                piece = lambda r, c=c, gi=gi: o_refs[gi][0, r, :, c * LANES:(c + 1) * LANES].astype(F32)
                _interleave_rows(so_ref.at[gi, c], tmp_ref.at[c], piece, dil)
            _interleave_rows(sl_ref.at[gi], tmp_ref.at[chunks], lambda r, gi=gi: l_refs[gi][0, r], dil)
            outs.append(jnp.concatenate([so_ref[gi, c] for c in range(chunks)], axis=-1))
            lses.append(sl_ref[gi])
    mx = jnp.maximum(jnp.maximum(lses[0], lses[1]), lses[2])
    es = [jnp.exp(l - mx) for l in lses]
    den = es[0] + es[1] + es[2]
    heads = []
    for hh in range(A_HEADS):
        cols = slice(hh * HEAD_DIM, (hh + 1) * HEAD_DIM)
        acc = None
        for g in range(ng):
            term = (es[g][:, hh:hh + 1] / den[:, hh:hh + 1]) * outs[g][:, cols]
            acc = term if acc is None else acc + term
        heads.append(acc)
    y_a = jnp.concatenate(heads, axis=-1).astype(BF16)
    merged = g_a * _dot(y_a, wpa_ref[...]) + g_b * z_b
    xn = x + _dot(merged.astype(BF16), wout_ref[...])
    xo_ref[...] = xn
    h2 = _rms(xn, gffn_ref[...])
    h2_ref[...] = h2.astype(h2_ref.dtype)
    if moe:
        rout_ref[...] = _top2(h2, wr_hi_ref[...], wr_lo_ref[...])


def _merge(x, outs, lses, y_b, seq, gmix, w_gate, b_gate, w_pa, w_pb, w_out, gffn, w_router=None):
    t = x.shape[0]
    tm = TOKEN_TILE
    moe = w_router is not None
    row = lambda i: (i, 0)
    fixed = lambda i: (0, 0)
    full = lambda a: pl.BlockSpec(a.shape, fixed)
    weights = [gmix, w_gate, b_gate, w_pa, w_pb, w_out, gffn]
    out_specs = [pl.BlockSpec((tm, D_MODEL), row), pl.BlockSpec((tm, D_MODEL), row)]
    out_shape = [jax.ShapeDtypeStruct((t, D_MODEL), F32), jax.ShapeDtypeStruct((t, D_MODEL), F32 if moe else BF16)]
    if moe:
        wr_hi, wr_lo = _split_bf16(w_router.T)
        weights.extend([jnp.concatenate([wr_hi, wr_lo]), jnp.concatenate([wr_hi, jnp.zeros_like(wr_hi)])])
        out_specs.append(pl.BlockSpec((tm, LANES), row))
        out_shape.append(jax.ShapeDtypeStruct((t, LANES), F32))
    ng = len(A_GROUPS)
    return pl.pallas_call(
        functools.partial(_merge_kernel, moe=moe),
        grid=(t // tm,),
        in_specs=[pl.BlockSpec((tm, D_MODEL), row)]
                 + [_residue_spec(seq, dil, A_WIDTH) for _, dil in A_GROUPS]
                 + [_residue_spec(seq, dil, LANES) for _, dil in A_GROUPS]
                 + [pl.BlockSpec((tm, B_Q_WIDTH), row)] + [full(a) for a in weights],
        out_specs=out_specs,
        out_shape=out_shape,
        scratch_shapes=[pltpu.VMEM((ng, A_WIDTH // LANES, tm, LANES), F32), pltpu.VMEM((ng, tm, LANES), F32),
                        pltpu.VMEM((A_WIDTH // LANES + 1, ROW_STRIDE, tm // ROW_STRIDE, LANES), F32)],
        compiler_params=_params("parallel"),
        name="merge",
    )(x, *outs, *lses, y_b, *weights)


def _swiglu_step(xb, wg, wu, wd):
    a = _dot(xb, wg)
    return _dot((jax.nn.silu(a) * _dot(xb, wu)).astype(BF16), wd)


def _ffn_kernel(h_ref, x_ref, wg_ref, wu_ref, wd_ref, o_ref, hm_ref):
    h = h_ref[...]
    d_ff = wg_ref.shape[1]
    for lo in range(0, d_ff, FF_CHUNK_DENSE):
        cols = slice(lo, min(lo + FF_CHUNK_DENSE, d_ff))
        hm_ref[:, cols] = (jax.nn.silu(_dot(h, wg_ref[:, cols])) * _dot(h, wu_ref[:, cols])).astype(BF16)
    o_ref[...] = x_ref[...] + _dot(hm_ref[...], wd_ref[...])


def _ffn(h2, x, w_g, w_u, w_d):
    t = x.shape[0]
    tm = TOKEN_TILE
    d_ff = w_g.shape[1]
    row = lambda i: (i, 0)
    fixed = lambda i: (0, 0)
    return pl.pallas_call(
        _ffn_kernel,
        grid=(t // tm,),
        in_specs=[pl.BlockSpec((tm, D_MODEL), row), pl.BlockSpec((tm, D_MODEL), row),
                  pl.BlockSpec((D_MODEL, d_ff), fixed), pl.BlockSpec((D_MODEL, d_ff), fixed),
                  pl.BlockSpec((d_ff, D_MODEL), fixed)],
        out_specs=pl.BlockSpec((tm, D_MODEL), row),
        out_shape=jax.ShapeDtypeStruct((t, D_MODEL), F32),
        scratch_shapes=[pltpu.VMEM((tm, d_ff), BF16)],
        compiler_params=_params("parallel"),
        name="ffn_dense",
    )(h2, x, w_g, w_u, w_d)


def _dispatch_kernel(zero_ref, dest_ref, h_ref, xs_ref, zbuf_ref, sem, zsem):
    tm = h_ref.shape[0]

    @pl.when(pl.program_id(0) == 0)
    def _():
        zbuf_ref[...] = jnp.zeros_like(zbuf_ref)
        n_zero = zero_ref.shape[0]
        zero_copy = lambda z: pltpu.make_async_copy(
            zbuf_ref, xs_ref.at[pl.ds(pl.multiple_of(zero_ref[z], MOE_TILE), MOE_TILE), :], zsem)
        for z in range(n_zero):
            zero_copy(z).start()
            zero_copy(z).wait()

    def body(r, c):
        for k in range(TOP_K):
            d = dest_ref[0, 0, TOP_K * r + k]
            pltpu.make_async_copy(h_ref.at[pl.ds(r, 1), :], xs_ref.at[pl.ds(d, 1), :], sem).start(priority=k)
        return c

    lax.fori_loop(0, tm, body, 0, unroll=ROW_LOOP_UNROLL)
    for k in range(TOP_K):
        pltpu.make_async_copy(h_ref, xs_ref.at[pl.ds(0, tm), :], sem).wait()


def _dispatch(h2, dest, zero_blocks, cap):
    t = h2.shape[0]
    tm = DISPATCH_TILE
    return pl.pallas_call(
        _dispatch_kernel,
        grid_spec=pltpu.PrefetchScalarGridSpec(
            num_scalar_prefetch=1,
            grid=(t // tm,),
            in_specs=[pl.BlockSpec((1, 1, TOP_K * tm), lambda i, z: (i, 0, 0), memory_space=pltpu.SMEM),
                      pl.BlockSpec((tm, D_MODEL), lambda i, z: (i, 0))],
            out_specs=pl.BlockSpec(memory_space=pl.ANY),
            scratch_shapes=[pltpu.VMEM((MOE_TILE, D_MODEL), F32), pltpu.SemaphoreType.DMA(()),
                            pltpu.SemaphoreType.DMA(())],
        ),
        out_shape=jax.ShapeDtypeStruct((cap, D_MODEL), F32),
        compiler_params=_params("arbitrary"),
        name="dispatch",
    )(zero_blocks, dest.reshape(t // tm, 1, TOP_K * tm), h2)


def _expert_kernel(be_ref, xs_ref, wg_ref, wu_ref, wd_ref, ys_ref, xb_ref, acc_ref):
    del be_ref
    j = pl.program_id(1)

    @pl.when(j == 0)
    def _():
        xb_ref[...] = xs_ref[...].astype(BF16)

    part = _swiglu_step(xb_ref[...], wg_ref[0], wu_ref[0], wd_ref[0])

    @pl.when(j == 0)
    def _():
        acc_ref[...] = part

    @pl.when(j > 0)
    def _():
        acc_ref[...] += part

    @pl.when(j == pl.num_programs(1) - 1)
    def _():
        ys_ref[...] = acc_ref[...]


def _experts(xs, block_e, w_g, w_u, w_d):
    cap = xs.shape[0]
    tm = MOE_TILE
    d_ff = w_g.shape[2]
    tf = d_ff // FF_SPLIT_EXPERT
    assert tf * FF_SPLIT_EXPERT == d_ff and tf % LANES == 0
    return pl.pallas_call(
        _expert_kernel,
        grid_spec=pltpu.PrefetchScalarGridSpec(
            num_scalar_prefetch=1,
            grid=(cap // tm, FF_SPLIT_EXPERT),
            in_specs=[pl.BlockSpec((tm, D_MODEL), lambda i, j, be: (i, 0)),
                      pl.BlockSpec((1, D_MODEL, tf), lambda i, j, be: (be[i], 0, j)),
                      pl.BlockSpec((1, D_MODEL, tf), lambda i, j, be: (be[i], 0, j)),
                      pl.BlockSpec((1, tf, D_MODEL), lambda i, j, be: (be[i], j, 0))],
            out_specs=pl.BlockSpec((tm, D_MODEL), lambda i, j, be: (i, 0)),
            scratch_shapes=[pltpu.VMEM((tm, D_MODEL), BF16), pltpu.VMEM((tm, D_MODEL), F32)],
        ),
        out_shape=jax.ShapeDtypeStruct((cap, D_MODEL), F32),
        compiler_params=_params("parallel", "arbitrary"),
        name="experts",
    )(block_e, xs, w_g, w_u, w_d)


def _combine_kernel(dest_ref, dest_next_ref, r_ref, x_ref, ys_ref, g_ref, o_ref, y_ref, sem):
    tm = x_ref.shape[0]
    i = pl.program_id(0)

    def fetch(idx_ref, slot):
        def body(r, c):
            for k in range(TOP_K):
                d = idx_ref[0, 0, TOP_K * r + k]
                pltpu.make_async_copy(ys_ref.at[pl.ds(d, 1), :], y_ref.at[slot, k, pl.ds(r, 1), :],
                                      sem.at[slot, k]).start(priority=k)
            return c
        lax.fori_loop(0, tm, body, 0, unroll=ROW_LOOP_UNROLL)

    @pl.when(i == 0)
    def _():
        fetch(dest_ref, 0)

    @pl.when(i + 1 < pl.num_programs(0))
    def _():
        fetch(dest_next_ref, (i + 1) % 2)

    slot = i % 2
    for k in range(TOP_K):
        pltpu.make_async_copy(ys_ref.at[pl.ds(0, tm), :], y_ref.at[slot, k], sem.at[slot, k]).wait()
    gates = r_ref[...]
    moe = gates[:, 2:3] * y_ref[slot, 0] + gates[:, 3:4] * y_ref[slot, 1]
    o_ref[...] = _rms(x_ref[...] + moe, g_ref[...])


def _combine(ys, dest, rout, x, g_final):
    t = x.shape[0]
    tm = COMBINE_TILE
    steps = t // tm
    dest3 = dest.reshape(steps, 1, TOP_K * tm)
    return pl.pallas_call(
        _combine_kernel,
        grid=(steps,),
        in_specs=[pl.BlockSpec((1, 1, TOP_K * tm), lambda i: (i, 0, 0), memory_space=pltpu.SMEM),
                  pl.BlockSpec((1, 1, TOP_K * tm), lambda i: (jnp.minimum(i + 1, steps - 1), 0, 0),
                               memory_space=pltpu.SMEM),
                  pl.BlockSpec((tm, LANES), lambda i: (i, 0)),
                  pl.BlockSpec((tm, D_MODEL), lambda i: (i, 0)),
                  pl.BlockSpec(memory_space=pl.ANY),
                  pl.BlockSpec((1, D_MODEL), lambda i: (0, 0))],
        out_specs=pl.BlockSpec((tm, D_MODEL), lambda i: (i, 0)),
        out_shape=jax.ShapeDtypeStruct((t, D_MODEL), F32),
        scratch_shapes=[pltpu.VMEM((2, TOP_K, tm, D_MODEL), F32), pltpu.SemaphoreType.DMA((2, TOP_K))],
        compiler_params=_params("arbitrary"),
        name="combine",
    )(dest3, dest3, rout, x, ys, g_final)


def _routing_tables(rout, t):
    tm = MOE_TILE
    e_flat = rout[:, :TOP_K].astype(jnp.int32).reshape(-1)
    onehot = (e_flat[:, None] == jnp.arange(N_EXPERTS, dtype=jnp.int32)[None, :]).astype(jnp.int32)
    csum = jnp.cumsum(onehot, axis=0)
    counts = csum[-1]
    padded = (counts + tm - 1) // tm * tm
    pend = jnp.cumsum(padded)
    pstart = pend - padded
    dest = jnp.sum(onehot * (csum - 1 + pstart[None, :]), axis=1).astype(jnp.int32)
    n_blocks = -(-(t * TOP_K) // tm) + N_EXPERTS
    block_start = jnp.arange(n_blocks, dtype=jnp.int32) * tm
    block_e = jnp.minimum(jnp.sum((pend[None, :] <= block_start[:, None]).astype(jnp.int32), axis=1), N_EXPERTS - 1)
    zero_blocks = jnp.concatenate([jnp.maximum(pend - tm, 0), block_start[n_blocks - N_EXPERTS:]]).astype(jnp.int32)
    return dest, block_e, zero_blocks, n_blocks * tm


def _trunk(x3, wts):
    bsz, seq, _ = x3.shape
    assert seq % TOKEN_TILE == 0
    x = x3.reshape(bsz * seq, D_MODEL)
    depth = len(wts["layers"])
    for li, lw in enumerate(wts["layers"]):
        *pas, qb, kvb = _proj(x, lw["norm_mix_g"], lw["w_in"], bsz, seq)
        outs, lses = zip(*[_attn_a(pa, gi) for gi, pa in enumerate(pas)])
        y_b = _attn_b(qb, kvb, lw["sink"], bsz, seq)
        mixer = (lw["norm_mix_g"], lw["w_gate"], lw["b_gate"], lw["w_proj_a"], lw["w_proj_b"], lw["w_out"],
                 lw["norm_ffn_g"])
        if "w_router" not in lw:
            assert li < depth - 1
            x, h2 = _merge(x, outs, lses, y_b, seq, *mixer)
            x = _ffn(h2, x, lw["w_ff_gate"], lw["w_ff_up"], lw["w_ff_down"])
        else:
            assert li == depth - 1
            x, h2, rout = _merge(x, outs, lses, y_b, seq, *mixer, w_router=lw["w_router"])
            dest, block_e, zero_blocks, cap = _routing_tables(rout, bsz * seq)
            xs = _dispatch(h2, dest, zero_blocks, cap)
            ys = _experts(xs, block_e, lw["w_e_gate"], lw["w_e_up"], lw["w_e_down"])
            x = _combine(ys, dest, rout, x, wts["norm_final_g"])
    return x.reshape(bsz, seq, D_MODEL)


def kernel(x_prompt, x_sample, norm_mix_g, w_in, w_gate, b_gate, w_proj_a, w_proj_b, w_out, sink, norm_ffn_g,
           w_ff_gate, w_ff_up, w_ff_down, w_router, w_e_gate, w_e_up, w_e_down, norm_final_g):
    depth = w_in.shape[0]
    layers = []
    for li in range(depth):
        lw = {
            "norm_mix_g": norm_mix_g[li].reshape(1, D_MODEL), "w_in": w_in[li].astype(BF16),
            "w_gate": w_gate[li].astype(BF16), "b_gate": b_gate[li].reshape(1, 2 * D_MODEL),
            "w_proj_a": w_proj_a[li].astype(BF16), "w_proj_b": w_proj_b[li].astype(BF16),
            "w_out": w_out[li].astype(BF16), "sink": sink[li], "norm_ffn_g": norm_ffn_g[li].reshape(1, D_MODEL),
        }
        i = li // 2
        if li % 2 == 0:
            lw.update(w_ff_gate=w_ff_gate[i].astype(BF16), w_ff_up=w_ff_up[i].astype(BF16),
                      w_ff_down=w_ff_down[i].astype(BF16))
        else:
            lw.update(w_router=w_router[i], w_e_gate=w_e_gate[i].astype(BF16), w_e_up=w_e_up[i].astype(BF16),
                      w_e_down=w_e_down[i].astype(BF16))
        layers.append(lw)
    wts = {"layers": layers, "norm_final_g": norm_final_g.reshape(1, D_MODEL)}
    return _trunk(x_prompt, wts), _trunk(x_sample, wts)
```

```python
import functools

import numpy as np
import jax
import jax.numpy as jnp
from jax import lax
from jax.experimental import pallas as pl
from jax.experimental.pallas import tpu as pltpu

F32 = jnp.float32
BF16 = jnp.bfloat16

D_MODEL = 1024
HEAD_DIM = 64
A_GROUPS = ((128, 1), (512, 4), (2048, 16))
A_HALF_WINDOW = 64
A_HEADS = 4
A_WIDTH = A_HEADS * HEAD_DIM
A_COLS = len(A_GROUPS) * 3 * A_WIDTH
B_Q_HEADS = 8
B_KV_HEADS = 2
B_GROUP = B_Q_HEADS // B_KV_HEADS
B_Q_WIDTH = B_Q_HEADS * HEAD_DIM
B_KV_WIDTH = B_KV_HEADS * HEAD_DIM
B_HALF_WINDOW = 128
N_IN = A_COLS + B_Q_WIDTH + 2 * B_KV_WIDTH
N_EXPERTS = 8
TOP_K = 2
RMS_EPS = 1e-6
NEG_INF = -1e30

LANES = 128
TOKEN_TILE = 512
ATTN_Q_TILE = 128
ATTN_A_ROWS = 2048
ATTN_A_BLOCKS = 4
FF_CHUNK_DENSE = 1536
FF_SPLIT_EXPERT = 2
MOE_TILE = 512
DISPATCH_TILE = 1024
COMBINE_TILE = 512
ATTN_B_BLOCKS = 2
ROW_STRIDE = 4
ROW_LOOP_UNROLL = 8
VMEM_LIMIT = 52 * 1024 * 1024


def _alibi_slopes(n):
    return np.asarray(2.0 ** (-8.0 * np.arange(1, n + 1) / n), dtype=np.float32)


def _params(*sem):
    return pltpu.CompilerParams(dimension_semantics=sem, vmem_limit_bytes=VMEM_LIMIT)


def _rms(x, g):
    return x * lax.rsqrt(jnp.mean(x * x, axis=-1, keepdims=True) + RMS_EPS) * g


def _dot(a, b):
    return jnp.dot(a, b, preferred_element_type=F32)


def _residue_spec(seq, dil, width):
    tiles_per_seq = seq // TOKEN_TILE
    return pl.BlockSpec((1, dil, TOKEN_TILE // dil, width),
                        lambda i: (i // tiles_per_seq, 0, i % tiles_per_seq, 0))


def _proj_kernel(x_ref, g_ref, w_ref, *refs):
    pa_refs, (qb_ref, kv_ref, scr_ref, tmp_ref) = refs[:len(A_GROUPS)], refs[len(A_GROUPS):]
    h = _rms(x_ref[...], g_ref[...]).astype(BF16)
    for gi, (_, dil) in enumerate(A_GROUPS):
        cols = slice(gi * 3 * A_WIDTH, (gi + 1) * 3 * A_WIDTH)
        res = _dot(h, w_ref[:, cols])
        if dil == 1:
            pa_refs[gi][0, 0] = res.astype(BF16)
        else:
            chunks = res.shape[1] // LANES
            for c in range(chunks):
                scr_ref[c] = res[:, c * LANES:(c + 1) * LANES]
            rows = TOKEN_TILE // dil
            if dil <= ROW_STRIDE:
                src = lambda c, r: scr_ref[c, pl.ds(r, rows, stride=dil), :]
            else:
                outer = dil // ROW_STRIDE
                for c in range(chunks):
                    for r1 in range(ROW_STRIDE):
                        tmp_ref[c, r1] = scr_ref[c, pl.ds(r1, TOKEN_TILE // ROW_STRIDE, stride=ROW_STRIDE), :]
                src = lambda c, r: tmp_ref[c, r % ROW_STRIDE, pl.ds(r // ROW_STRIDE, rows, stride=outer), :]
            for r in range(dil):
                parts = [src(c, r) for c in range(chunks)]
                pa_refs[gi][0, r] = jnp.concatenate(parts, axis=-1).astype(BF16)
    qb_ref[...] = _dot(h, w_ref[:, A_COLS:A_COLS + B_Q_WIDTH]).astype(BF16)
    kv_ref[...] = _dot(h, w_ref[:, A_COLS + B_Q_WIDTH:]).astype(BF16)


def _proj(x, g, w_in, bsz, seq):
    t = x.shape[0]
    tm = TOKEN_TILE
    row = lambda i: (i, 0)
    fixed = lambda i: (0, 0)
    gw = 3 * A_WIDTH
    return pl.pallas_call(
        _proj_kernel,
        grid=(t // tm,),
        in_specs=[pl.BlockSpec((tm, D_MODEL), row), pl.BlockSpec((1, D_MODEL), fixed),
                  pl.BlockSpec((D_MODEL, N_IN), fixed)],
        out_specs=[_residue_spec(seq, dil, gw) for _, dil in A_GROUPS]
                  + [pl.BlockSpec((tm, B_Q_WIDTH), row), pl.BlockSpec((tm, 2 * B_KV_WIDTH), row)],
        out_shape=[jax.ShapeDtypeStruct((bsz, dil, seq // dil, gw), BF16) for _, dil in A_GROUPS]
                  + [jax.ShapeDtypeStruct((t, B_Q_WIDTH), BF16), jax.ShapeDtypeStruct((t, 2 * B_KV_WIDTH), BF16)],
        scratch_shapes=[pltpu.VMEM((gw // LANES, tm, LANES), F32),
                        pltpu.VMEM((gw // LANES, ROW_STRIDE, tm // ROW_STRIDE, LANES), F32)],
        compiler_params=_params("parallel"),
        name="proj",
    )(x, g, w_in)


BAND_CASES = 3


def _band(q0, n, kw, half_window):
    start = pl.multiple_of(jnp.clip(q0 - half_window, 0, n - kw), half_window)
    return start, (q0 - start) // half_window


def _fill_bias(bias_ref, slopes, tq, kw, half_window):
    rel = lax.broadcasted_iota(jnp.int32, (tq, kw), 0) - lax.broadcasted_iota(jnp.int32, (tq, kw), 1)
    for case in range(BAND_CASES):
        dist = jnp.abs(rel + case * half_window)
        for h, slope in enumerate(slopes):
            bias_ref[case, h] = jnp.where(dist <= half_window, -slope * dist.astype(F32), NEG_INF)


Q_SCALE = HEAD_DIM ** -0.5


def _head_scores(qh, kh, bias):
    return lax.dot_general(qh, kh, (((1,), (1,)), ((), ())), preferred_element_type=F32) + bias


def _attn_a_shapes(n, dil):
    tq = min(ATTN_Q_TILE, n)
    res = max(1, min(dil, ATTN_A_ROWS // n))
    return tq, min(n, tq + 2 * A_HALF_WINDOW), res, min(ATTN_A_BLOCKS, res * (n // tq))


def _attn_a_kernel(x_ref, o_ref, lse_ref, s_ref, p_ref, bias_ref, *, n, dil, slopes):
    tq, kw, res, nb = _attn_a_shapes(n, dil)
    per_res = n // tq
    lane = lax.broadcasted_iota(jnp.int32, (tq, LANES), 1)
    lane_head = lax.broadcasted_iota(jnp.int32, (tq, A_WIDTH), 1) // HEAD_DIM
    n_pairs = nb * A_HEADS

    @pl.when((pl.program_id(0) == 0) & (pl.program_id(1) == 0))
    def _():
        _fill_bias(bias_ref, slopes, tq, kw, A_HALF_WINDOW)

    def step(it, carry):
        blocks = []
        for j in range(nb):
            blk = it * nb + j
            r = blk // per_res
            q0 = pl.multiple_of((blk % per_res) * tq, tq)
            start, case = _band(q0, n, kw, A_HALF_WINDOW)
            blocks.append((r, q0, case, x_ref[0, r, pl.ds(q0, tq), 0:A_WIDTH] * Q_SCALE,
                           x_ref[0, r, pl.ds(start, kw), A_WIDTH:2 * A_WIDTH],
                           x_ref[0, r, pl.ds(start, kw), 2 * A_WIDTH:3 * A_WIDTH]))
        for j, (_, _, case, q, k, _) in enumerate(blocks):
            q_heads = jnp.concatenate([jnp.where(lane_head == h, q, jnp.zeros_like(q)) for h in range(A_HEADS)],
                                      axis=0)
            s_all = lax.dot_general(q_heads, k, (((1,), (1,)), ((), ())), preferred_element_type=F32)
            for h in range(A_HEADS):
                s_ref[j * A_HEADS + h] = s_all[h * tq:(h + 1) * tq] + bias_ref[case, h]
        ms = [jnp.max(s_ref[i], axis=-1, keepdims=True) for i in range(n_pairs)]
        ls = []
        for i in range(n_pairs):
            p = jnp.exp(s_ref[i] - ms[i])
            ls.append(jnp.sum(p, axis=-1, keepdims=True))
            p_ref[i] = p.astype(BF16)
        for j in range(nb):
            r, q0, v = blocks[j][0], blocks[j][1], blocks[j][5]
            pv = _dot(p_ref[pl.ds(j * A_HEADS, A_HEADS)].reshape(A_HEADS * tq, kw), v)
            out = jnp.zeros((tq, A_WIDTH), F32)
            lse_tile = jnp.zeros((tq, LANES), F32)
            for h in range(A_HEADS):
                i = j * A_HEADS + h
                out = jnp.where(lane_head == h, pv[h * tq:(h + 1) * tq] / ls[i], out)
                lse_tile = jnp.where(lane == h, ms[i] + jnp.log(ls[i]), lse_tile)
            o_ref[0, r, pl.ds(q0, tq), :] = out.astype(BF16)
            lse_ref[0, r, pl.ds(q0, tq), :] = lse_tile
        return carry

    lax.fori_loop(0, res * per_res // nb, step, 0)


def _attn_a(pa, gi):
    window, dil = A_GROUPS[gi]
    assert window // (2 * dil) == A_HALF_WINDOW
    bsz, _, n, gw = pa.shape
    tq, kw, res, nb = _attn_a_shapes(n, dil)
    assert dil % res == 0 and (res * (n // tq)) % nb == 0
    slopes = _alibi_slopes(len(A_GROUPS) * A_HEADS).reshape(len(A_GROUPS), A_HEADS)[gi]
    slopes = tuple(float(s * np.float32(dil)) for s in slopes)
    block = lambda w: pl.BlockSpec((1, res, n, w), lambda b, r: (b, r, 0, 0))
    return pl.pallas_call(
        functools.partial(_attn_a_kernel, n=n, dil=dil, slopes=slopes),
        grid=(bsz, dil // res),
        in_specs=[block(gw)],
        out_specs=[block(A_WIDTH), block(LANES)],
        out_shape=[jax.ShapeDtypeStruct((bsz, dil, n, A_WIDTH), BF16),
                   jax.ShapeDtypeStruct((bsz, dil, n, LANES), F32)],
        scratch_shapes=[pltpu.VMEM((nb * A_HEADS, tq, kw), F32), pltpu.VMEM((nb * A_HEADS, tq, kw), BF16),
                        pltpu.VMEM((BAND_CASES, A_HEADS, tq, kw), F32)],
        compiler_params=_params("arbitrary", "arbitrary"),
        name=f"attn_a{gi}",
    )(pa)


def _attn_b_kernel(sink_ref, q_ref, kv_ref, o_ref, s_ref, p_ref, bias_ref, *, n, slopes):
    hw = B_HALF_WINDOW
    tq = min(ATTN_Q_TILE, n)
    kw = min(n, tq + 2 * hw)
    heads = range(B_Q_HEADS)
    low = lax.broadcasted_iota(jnp.int32, (tq, LANES), 1) < HEAD_DIM
    swap = lambda t: jnp.concatenate([t[:, HEAD_DIM:], t[:, :HEAD_DIM]], axis=1)

    @pl.when(pl.program_id(0) == 0)
    def _():
        _fill_bias(bias_ref, slopes, tq, kw, hw)

    nb = min(ATTN_B_BLOCKS, n // tq)
    pairs = [(j, hq) for j in range(nb) for hq in heads]

    def step(it, carry):
        blocks = []
        for j in range(nb):
            q0 = pl.multiple_of((it * nb + j) * tq, tq)
            start, case = _band(q0, n, kw, hw)
            blocks.append((q0, case, q_ref[0, pl.ds(q0, tq), :] * Q_SCALE,
                           kv_ref[0, pl.ds(start, kw), 0:B_KV_WIDTH],
                           kv_ref[0, pl.ds(start, kw), B_KV_WIDTH:2 * B_KV_WIDTH]))
        for j, (_, case, q, k2, _) in enumerate(blocks):
            for hk in range(B_KV_HEADS):
                rows = []
                for hq in range(hk * B_GROUP, (hk + 1) * B_GROUP):
                    tile = q[:, (hq // 2) * LANES:(hq // 2 + 1) * LANES]
                    moved = tile if hq % 2 == hk else swap(tile)
                    rows.append(jnp.where(low == (hk == 0), moved, jnp.zeros_like(moved)))
                s_all = lax.dot_general(jnp.concatenate(rows, axis=0), k2, (((1,), (1,)), ((), ())),
                                        preferred_element_type=F32)
                for g in range(B_GROUP):
                    hq = hk * B_GROUP + g
                    s_ref[j * B_Q_HEADS + hq] = s_all[g * tq:(g + 1) * tq] + bias_ref[case, hq]
        ms = [jnp.maximum(jnp.max(s_ref[i], axis=-1, keepdims=True), sink_ref[hq]) for i, (_, hq) in enumerate(pairs)]
        ls = []
        for i, (_, hq) in enumerate(pairs):
            p = jnp.exp(s_ref[i] - ms[i])
            ls.append(jnp.sum(p, axis=-1, keepdims=True) + jnp.exp(sink_ref[hq] - ms[i]))
            p_ref[i] = p.astype(BF16)
        for j, (q0, _, _, _, v2) in enumerate(blocks):
            placed = []
            for hk in range(B_KV_HEADS):
                first = j * B_Q_HEADS + hk * B_GROUP
                pv = _dot(p_ref[pl.ds(first, B_GROUP)].reshape(B_GROUP * tq, kw), v2)
                for g in range(B_GROUP):
                    hq = hk * B_GROUP + g
                    o = pv[g * tq:(g + 1) * tq] / ls[first + g]
                    placed.append(o if hq % 2 == hk else swap(o))
            tiles = [jnp.where(low, placed[t], placed[t + 1]) for t in range(0, B_Q_HEADS, 2)]
            o_ref[0, pl.ds(q0, tq), :] = jnp.concatenate(tiles, axis=1).astype(BF16)
        return carry

    lax.fori_loop(0, n // (tq * nb), step, 0)


def _attn_b(qb, kvb, sink, bsz, seq):
    slopes = tuple(float(s) for s in _alibi_slopes(B_Q_HEADS))
    tq = min(ATTN_Q_TILE, seq)
    kw = min(seq, tq + 2 * B_HALF_WINDOW)
    nb = min(ATTN_B_BLOCKS, seq // tq)
    assert seq % (tq * nb) == 0
    out = pl.pallas_call(
        functools.partial(_attn_b_kernel, n=seq, slopes=slopes),
        grid=(bsz,),
        in_specs=[pl.BlockSpec(memory_space=pltpu.SMEM),
                  pl.BlockSpec((1, seq, B_Q_WIDTH), lambda b: (b, 0, 0)),
                  pl.BlockSpec((1, seq, 2 * B_KV_WIDTH), lambda b: (b, 0, 0))],
        out_specs=pl.BlockSpec((1, seq, B_Q_WIDTH), lambda b: (b, 0, 0)),
        out_shape=jax.ShapeDtypeStruct((bsz, seq, B_Q_WIDTH), BF16),
        scratch_shapes=[pltpu.VMEM((nb * B_Q_HEADS, tq, kw), F32), pltpu.VMEM((nb * B_Q_HEADS, tq, kw), BF16),
                        pltpu.VMEM((BAND_CASES, B_Q_HEADS, tq, kw), F32)],
        compiler_params=_params("arbitrary"),
        name="attn_b",
    )(sink, qb.reshape(bsz, seq, B_Q_WIDTH), kvb.reshape(bsz, seq, 2 * B_KV_WIDTH))
    return out.reshape(bsz * seq, B_Q_WIDTH)


def _split_bf16(a):
    hi = a.astype(BF16)
    return hi, (a - hi.astype(F32)).astype(BF16)


def _top2(h2, wr_a, wr_b):
    tm = h2.shape[0]
    nt = (((1,), (1,)), ((), ()))
    h_hi, h_lo = _split_bf16(h2)
    by_hi = lax.dot_general(wr_a, h_hi, nt, preferred_element_type=F32)
    by_lo = lax.dot_general(wr_b, h_lo, nt, preferred_element_type=F32)
    logits = by_hi[:N_EXPERTS] + (by_hi[N_EXPERTS:] + by_lo[:N_EXPERTS])
    row = lax.broadcasted_iota(jnp.int32, logits.shape, 0)
    m1 = jnp.max(logits, axis=0, keepdims=True)
    i1 = jnp.min(jnp.where(logits == m1, row, N_EXPERTS), axis=0, keepdims=True)
    l2 = jnp.where(row == i1, -jnp.inf, logits)
    m2 = jnp.max(l2, axis=0, keepdims=True)
    i2 = jnp.min(jnp.where(l2 == m2, row, N_EXPERTS), axis=0, keepdims=True)
    e2 = jnp.exp(m2 - m1)
    den = 1.0 + e2
    packed = jnp.where(row == 0, i1.astype(F32),
                       jnp.where(row == 1, i2.astype(F32),
                                 jnp.where(row == 2, 1.0 / den, jnp.where(row == 3, e2 / den, 0.0))))
    padded = jnp.concatenate([packed, jnp.zeros((LANES - N_EXPERTS, tm), F32)], axis=0)
    return padded.T


def _interleave_rows(dst_ref, tmp_ref, piece, dil):
    rows = TOKEN_TILE // dil
    if dil <= ROW_STRIDE:
        for r in range(dil):
            dst_ref[pl.ds(r, rows, stride=dil), :] = piece(r)
        return
    outer = dil // ROW_STRIDE
    for r in range(dil):
        tmp_ref[r % ROW_STRIDE, pl.ds(r // ROW_STRIDE, rows, stride=outer), :] = piece(r)
    for r1 in range(ROW_STRIDE):
        dst_ref[pl.ds(r1, TOKEN_TILE // ROW_STRIDE, stride=ROW_STRIDE), :] = tmp_ref[r1]


def _merge_kernel(*refs, moe):
    ng = len(A_GROUPS)
    x_ref = refs[0]
    o_refs, l_refs = refs[1:1 + ng], refs[1 + ng:1 + 2 * ng]
    yb_ref, gmix_ref, wgate_ref, bgate_ref, wpa_ref, wpb_ref, wout_ref, gffn_ref = refs[1 + 2 * ng:9 + 2 * ng]
    rest = refs[9 + 2 * ng:]
    if moe:
        wr_hi_ref, wr_lo_ref, xo_ref, h2_ref, rout_ref, so_ref, sl_ref, tmp_ref = rest
    else:
        xo_ref, h2_ref, so_ref, sl_ref, tmp_ref = rest
    x = x_ref[...]
    h = _rms(x, gmix_ref[...]).astype(BF16)
    z_b = _dot(yb_ref[...], wpb_ref[...])
    g_a = jax.nn.sigmoid(_dot(h, wgate_ref[:, :D_MODEL]) + bgate_ref[:, :D_MODEL])
    g_b = jax.nn.sigmoid(_dot(h, wgate_ref[:, D_MODEL:]) + bgate_ref[:, D_MODEL:])
    outs, lses = [], []
    for gi, (_, dil) in enumerate(A_GROUPS):
        if dil == 1:
            outs.append(o_refs[gi][0, 0].astype(F32))
            lses.append(l_refs[gi][0, 0])
        else:
            chunks = A_WIDTH // LANES
            for c in range(chunks):
                piece = lambda r, c=c, gi=gi: o_refs[gi][0, r, :, c * LANES:(c + 1) * LANES].astype(F32)
                _interleave_rows(so_ref.at[gi, c], tmp_ref.at[c], piece, dil)
            _interleave_rows(sl_ref.at[gi], tmp_ref.at[chunks], lambda r, gi=gi: l_refs[gi][0, r], dil)
            outs.append(jnp.concatenate([so_ref[gi, c] for c in range(chunks)], axis=-1))
            lses.append(sl_ref[gi])
    mx = jnp.maximum(jnp.maximum(lses[0], lses[1]), lses[2])
    es = [jnp.exp(l - mx) for l in lses]
    den = es[0] + es[1] + es[2]
    heads = []
    for hh in range(A_HEADS):
        cols = slice(hh * HEAD_DIM, (hh + 1) * HEAD_DIM)
        acc = None
        for g in range(ng):
            term = (es[g][:, hh:hh + 1] / den[:, hh:hh + 1]) * outs[g][:, cols]
            acc = term if acc is None else acc + term
        heads.append(acc)
    y_a = jnp.concatenate(heads, axis=-1).astype(BF16)
    merged = g_a * _dot(y_a, wpa_ref[...]) + g_b * z_b
    xn = x + _dot(merged.astype(BF16), wout_ref[...])
    xo_ref[...] = xn
    h2 = _rms(xn, gffn_ref[...])
    h2_ref[...] = h2.astype(h2_ref.dtype)
    if moe:
        rout_ref[...] = _top2(h2, wr_hi_ref[...], wr_lo_ref[...])


def _merge(x, outs, lses, y_b, seq, gmix, w_gate, b_gate, w_pa, w_pb, w_out, gffn, w_router=None):
    t = x.shape[0]
    tm = TOKEN_TILE
    moe = w_router is not None
    row = lambda i: (i, 0)
    fixed = lambda i: (0, 0)
    full = lambda a: pl.BlockSpec(a.shape, fixed)
    weights = [gmix, w_gate, b_gate, w_pa, w_pb, w_out, gffn]
    out_specs = [pl.BlockSpec((tm, D_MODEL), row), pl.BlockSpec((tm, D_MODEL), row)]
    out_shape = [jax.ShapeDtypeStruct((t, D_MODEL), F32), jax.ShapeDtypeStruct((t, D_MODEL), F32 if moe else BF16)]
    if moe:
        wr_hi, wr_lo = _split_bf16(w_router.T)
        weights.extend([jnp.concatenate([wr_hi, wr_lo]), jnp.concatenate([wr_hi, jnp.zeros_like(wr_hi)])])
        out_specs.append(pl.BlockSpec((tm, LANES), row))
        out_shape.append(jax.ShapeDtypeStruct((t, LANES), F32))
    ng = len(A_GROUPS)
    return pl.pallas_call(
        functools.partial(_merge_kernel, moe=moe),
        grid=(t // tm,),
        in_specs=[pl.BlockSpec((tm, D_MODEL), row)]
                 + [_residue_spec(seq, dil, A_WIDTH) for _, dil in A_GROUPS]
                 + [_residue_spec(seq, dil, LANES) for _, dil in A_GROUPS]
                 + [pl.BlockSpec((tm, B_Q_WIDTH), row)] + [full(a) for a in weights],
        out_specs=out_specs,
        out_shape=out_shape,
        scratch_shapes=[pltpu.VMEM((ng, A_WIDTH // LANES, tm, LANES), F32), pltpu.VMEM((ng, tm, LANES), F32),
                        pltpu.VMEM((A_WIDTH // LANES + 1, ROW_STRIDE, tm // ROW_STRIDE, LANES), F32)],
        compiler_params=_params("parallel"),
        name="merge",
    )(x, *outs, *lses, y_b, *weights)


def _swiglu_step(xb, wg, wu, wd):
    a = _dot(xb, wg)
    return _dot((jax.nn.silu(a) * _dot(xb, wu)).astype(BF16), wd)


def _ffn_kernel(h_ref, x_ref, wg_ref, wu_ref, wd_ref, o_ref, hm_ref):
    h = h_ref[...]
    d_ff = wg_ref.shape[1]
    for lo in range(0, d_ff, FF_CHUNK_DENSE):
        cols = slice(lo, min(lo + FF_CHUNK_DENSE, d_ff))
        hm_ref[:, cols] = (jax.nn.silu(_dot(h, wg_ref[:, cols])) * _dot(h, wu_ref[:, cols])).astype(BF16)
    o_ref[...] = x_ref[...] + _dot(hm_ref[...], wd_ref[...])


def _ffn(h2, x, w_g, w_u, w_d):
    t = x.shape[0]
    tm = TOKEN_TILE
    d_ff = w_g.shape[1]
    row = lambda i: (i, 0)
    fixed = lambda i: (0, 0)
    return pl.pallas_call(
        _ffn_kernel,
        grid=(t // tm,),
        in_specs=[pl.BlockSpec((tm, D_MODEL), row), pl.BlockSpec((tm, D_MODEL), row),
                  pl.BlockSpec((D_MODEL, d_ff), fixed), pl.BlockSpec((D_MODEL, d_ff), fixed),
                  pl.BlockSpec((d_ff, D_MODEL), fixed)],
        out_specs=pl.BlockSpec((tm, D_MODEL), row),
        out_shape=jax.ShapeDtypeStruct((t, D_MODEL), F32),
        scratch_shapes=[pltpu.VMEM((tm, d_ff), BF16)],
        compiler_params=_params("parallel"),
        name="ffn_dense",
    )(h2, x, w_g, w_u, w_d)


def _dispatch_kernel(zero_ref, dest_ref, h_ref, xs_ref, zbuf_ref, sem, zsem):
    tm = h_ref.shape[0]

    @pl.when(pl.program_id(0) == 0)
    def _():
        zbuf_ref[...] = jnp.zeros_like(zbuf_ref)
        n_zero = zero_ref.shape[0]
        zero_copy = lambda z: pltpu.make_async_copy(
            zbuf_ref, xs_ref.at[pl.ds(pl.multiple_of(zero_ref[z], MOE_TILE), MOE_TILE), :], zsem)
        for z in range(n_zero):
            zero_copy(z).start()
            zero_copy(z).wait()

    def body(r, c):
        for k in range(TOP_K):
            d = dest_ref[0, 0, TOP_K * r + k]
            pltpu.make_async_copy(h_ref.at[pl.ds(r, 1), :], xs_ref.at[pl.ds(d, 1), :], sem).start(priority=k)
        return c

    lax.fori_loop(0, tm, body, 0, unroll=ROW_LOOP_UNROLL)
    for k in range(TOP_K):
        pltpu.make_async_copy(h_ref, xs_ref.at[pl.ds(0, tm), :], sem).wait()


def _dispatch(h2, dest, zero_blocks, cap):
    t = h2.shape[0]
    tm = DISPATCH_TILE
    return pl.pallas_call(
        _dispatch_kernel,
        grid_spec=pltpu.PrefetchScalarGridSpec(
            num_scalar_prefetch=1,
            grid=(t // tm,),
            in_specs=[pl.BlockSpec((1, 1, TOP_K * tm), lambda i, z: (i, 0, 0), memory_space=pltpu.SMEM),
                      pl.BlockSpec((tm, D_MODEL), lambda i, z: (i, 0))],
            out_specs=pl.BlockSpec(memory_space=pl.ANY),
            scratch_shapes=[pltpu.VMEM((MOE_TILE, D_MODEL), F32), pltpu.SemaphoreType.DMA(()),
                            pltpu.SemaphoreType.DMA(())],
        ),
        out_shape=jax.ShapeDtypeStruct((cap, D_MODEL), F32),
        compiler_params=_params("arbitrary"),
        name="dispatch",
    )(zero_blocks, dest.reshape(t // tm, 1, TOP_K * tm), h2)


def _expert_kernel(be_ref, xs_ref, wg_ref, wu_ref, wd_ref, ys_ref, xb_ref, acc_ref):
    del be_ref
    j = pl.program_id(1)

    @pl.when(j == 0)
    def _():
        xb_ref[...] = xs_ref[...].astype(BF16)

    part = _swiglu_step(xb_ref[...], wg_ref[0], wu_ref[0], wd_ref[0])

    @pl.when(j == 0)
    def _():
        acc_ref[...] = part

    @pl.when(j > 0)
    def _():
        acc_ref[...] += part

    @pl.when(j == pl.num_programs(1) - 1)
    def _():
        ys_ref[...] = acc_ref[...]


def _experts(xs, block_e, w_g, w_u, w_d):
    cap = xs.shape[0]
    tm = MOE_TILE
    d_ff = w_g.shape[2]
    tf = d_ff // FF_SPLIT_EXPERT
    assert tf * FF_SPLIT_EXPERT == d_ff and tf % LANES == 0
    return pl.pallas_call(
        _expert_kernel,
        grid_spec=pltpu.PrefetchScalarGridSpec(
            num_scalar_prefetch=1,
            grid=(cap // tm, FF_SPLIT_EXPERT),
            in_specs=[pl.BlockSpec((tm, D_MODEL), lambda i, j, be: (i, 0)),
                      pl.BlockSpec((1, D_MODEL, tf), lambda i, j, be: (be[i], 0, j)),
                      pl.BlockSpec((1, D_MODEL, tf), lambda i, j, be: (be[i], 0, j)),
                      pl.BlockSpec((1, tf, D_MODEL), lambda i, j, be: (be[i], j, 0))],
            out_specs=pl.BlockSpec((tm, D_MODEL), lambda i, j, be: (i, 0)),
            scratch_shapes=[pltpu.VMEM((tm, D_MODEL), BF16), pltpu.VMEM((tm, D_MODEL), F32)],
        ),
        out_shape=jax.ShapeDtypeStruct((cap, D_MODEL), F32),
        compiler_params=_params("parallel", "arbitrary"),
        name="experts",
    )(block_e, xs, w_g, w_u, w_d)


def _combine_kernel(dest_ref, dest_next_ref, r_ref, x_ref, ys_ref, g_ref, o_ref, y_ref, sem):
    tm = x_ref.shape[0]
    i = pl.program_id(0)

    def fetch(idx_ref, slot):
        def body(r, c):
            for k in range(TOP_K):
                d = idx_ref[0, 0, TOP_K * r + k]
                pltpu.make_async_copy(ys_ref.at[pl.ds(d, 1), :], y_ref.at[slot, k, pl.ds(r, 1), :],
                                      sem.at[slot, k]).start(priority=k)
            return c
        lax.fori_loop(0, tm, body, 0, unroll=ROW_LOOP_UNROLL)

    @pl.when(i == 0)
    def _():
        fetch(dest_ref, 0)

    @pl.when(i + 1 < pl.num_programs(0))
    def _():
        fetch(dest_next_ref, (i + 1) % 2)

    slot = i % 2
    for k in range(TOP_K):
        pltpu.make_async_copy(ys_ref.at[pl.ds(0, tm), :], y_ref.at[slot, k], sem.at[slot, k]).wait()
    gates = r_ref[...]
    moe = gates[:, 2:3] * y_ref[slot, 0] + gates[:, 3:4] * y_ref[slot, 1]
    o_ref[...] = _rms(x_ref[...] + moe, g_ref[...])


def _combine(ys, dest, rout, x, g_final):
    t = x.shape[0]
    tm = COMBINE_TILE
    steps = t // tm
    dest3 = dest.reshape(steps, 1, TOP_K * tm)
    return pl.pallas_call(
        _combine_kernel,
        grid=(steps,),
        in_specs=[pl.BlockSpec((1, 1, TOP_K * tm), lambda i: (i, 0, 0), memory_space=pltpu.SMEM),
                  pl.BlockSpec((1, 1, TOP_K * tm), lambda i: (jnp.minimum(i + 1, steps - 1), 0, 0),
                               memory_space=pltpu.SMEM),
                  pl.BlockSpec((tm, LANES), lambda i: (i, 0)),
                  pl.BlockSpec((tm, D_MODEL), lambda i: (i, 0)),
                  pl.BlockSpec(memory_space=pl.ANY),
                  pl.BlockSpec((1, D_MODEL), lambda i: (0, 0))],
        out_specs=pl.BlockSpec((tm, D_MODEL), lambda i: (i, 0)),
        out_shape=jax.ShapeDtypeStruct((t, D_MODEL), F32),
        scratch_shapes=[pltpu.VMEM((2, TOP_K, tm, D_MODEL), F32), pltpu.SemaphoreType.DMA((2, TOP_K))],
        compiler_params=_params("arbitrary"),
        name="combine",
    )(dest3, dest3, rout, x, ys, g_final)


def _routing_tables(rout, t):
    tm = MOE_TILE
    e_flat = rout[:, :TOP_K].astype(jnp.int32).reshape(-1)
    onehot = (e_flat[:, None] == jnp.arange(N_EXPERTS, dtype=jnp.int32)[None, :]).astype(jnp.int32)
    csum = jnp.cumsum(onehot, axis=0)
    counts = csum[-1]
    padded = (counts + tm - 1) // tm * tm
    pend = jnp.cumsum(padded)
    pstart = pend - padded
    dest = jnp.sum(onehot * (csum - 1 + pstart[None, :]), axis=1).astype(jnp.int32)
    n_blocks = -(-(t * TOP_K) // tm) + N_EXPERTS
    block_start = jnp.arange(n_blocks, dtype=jnp.int32) * tm
    block_e = jnp.minimum(jnp.sum((pend[None, :] <= block_start[:, None]).astype(jnp.int32), axis=1), N_EXPERTS - 1)
    zero_blocks = jnp.concatenate([jnp.maximum(pend - tm, 0), block_start[n_blocks - N_EXPERTS:]]).astype(jnp.int32)
    return dest, block_e, zero_blocks, n_blocks * tm


def _trunk(x3, wts):
    bsz, seq, _ = x3.shape
    assert seq % TOKEN_TILE == 0
    x = x3.reshape(bsz * seq, D_MODEL)
    depth = len(wts["layers"])
    for li, lw in enumerate(wts["layers"]):
        *pas, qb, kvb = _proj(x, lw["norm_mix_g"], lw["w_in"], bsz, seq)
        outs, lses = zip(*[_attn_a(pa, gi) for gi, pa in enumerate(pas)])
        y_b = _attn_b(qb, kvb, lw["sink"], bsz, seq)
        mixer = (lw["norm_mix_g"], lw["w_gate"], lw["b_gate"], lw["w_proj_a"], lw["w_proj_b"], lw["w_out"],
                 lw["norm_ffn_g"])
        if "w_router" not in lw:
            assert li < depth - 1
            x, h2 = _merge(x, outs, lses, y_b, seq, *mixer)
            x = _ffn(h2, x, lw["w_ff_gate"], lw["w_ff_up"], lw["w_ff_down"])
        else:
            assert li == depth - 1
            x, h2, rout = _merge(x, outs, lses, y_b, seq, *mixer, w_router=lw["w_router"])
            dest, block_e, zero_blocks, cap = _routing_tables(rout, bsz * seq)
            xs = _dispatch(h2, dest, zero_blocks, cap)
            ys = _experts(xs, block_e, lw["w_e_gate"], lw["w_e_up"], lw["w_e_down"])
            x = _combine(ys, dest, rout, x, wts["norm_final_g"])
    return x.reshape(bsz, seq, D_MODEL)


def kernel(x_prompt, x_sample, norm_mix_g, w_in, w_gate, b_gate, w_proj_a, w_proj_b, w_out, sink, norm_ffn_g,
           w_ff_gate, w_ff_up, w_ff_down, w_router, w_e_gate, w_e_up, w_e_down, norm_final_g):
    depth = w_in.shape[0]
    layers = []
    for li in range(depth):
        lw = {
            "norm_mix_g": norm_mix_g[li].reshape(1, D_MODEL), "w_in": w_in[li].astype(BF16),
            "w_gate": w_gate[li].astype(BF16), "b_gate": b_gate[li].reshape(1, 2 * D_MODEL),
            "w_proj_a": w_proj_a[li].astype(BF16), "w_proj_b": w_proj_b[li].astype(BF16),
            "w_out": w_out[li].astype(BF16), "sink": sink[li], "norm_ffn_g": norm_ffn_g[li].reshape(1, D_MODEL),
        }
        i = li // 2
        if li % 2 == 0:
            lw.update(w_ff_gate=w_ff_gate[i].astype(BF16), w_ff_up=w_ff_up[i].astype(BF16),
                      w_ff_down=w_ff_down[i].astype(BF16))
        else:
            lw.update(w_router=w_router[i], w_e_gate=w_e_gate[i].astype(BF16), w_e_up=w_e_up[i].astype(BF16),
                      w_e_down=w_e_down[i].astype(BF16))
        layers.append(lw)
    wts = {"layers": layers, "norm_final_g": norm_final_g.reshape(1, D_MODEL)}
    return _trunk(x_prompt, wts), _trunk(x_sample, wts)
```

```python
import functools

import numpy as np
import jax
import jax.numpy as jnp
from jax import lax
from jax.experimental import pallas as pl
from jax.experimental.pallas import tpu as pltpu

F32 = jnp.float32
BF16 = jnp.bfloat16

D_MODEL = 1024
HEAD_DIM = 64
A_GROUPS = ((128, 1), (512, 4), (2048, 16))
A_HALF_WINDOW = 64
A_HEADS = 4
A_WIDTH = A_HEADS * HEAD_DIM
A_COLS = len(A_GROUPS) * 3 * A_WIDTH
B_Q_HEADS = 8
B_KV_HEADS = 2
B_GROUP = B_Q_HEADS // B_KV_HEADS
B_Q_WIDTH = B_Q_HEADS * HEAD_DIM
B_KV_WIDTH = B_KV_HEADS * HEAD_DIM
B_HALF_WINDOW = 128
N_IN = A_COLS + B_Q_WIDTH + 2 * B_KV_WIDTH
N_EXPERTS = 8
TOP_K = 2
RMS_EPS = 1e-6
NEG_INF = -1e30

LANES = 128
TOKEN_TILE = 512
ATTN_Q_TILE = 128
ATTN_A_ROWS = 2048
ATTN_A_BLOCKS = 4
FF_CHUNK_DENSE = 1536
FF_SPLIT_EXPERT = 2
MOE_TILE = 512
DISPATCH_TILE = 1024
COMBINE_TILE = 512
ATTN_B_BLOCKS = 2
ROW_STRIDE = 4
ROW_LOOP_UNROLL = 8
VMEM_LIMIT = 52 * 1024 * 1024


def _alibi_slopes(n):
    return np.asarray(2.0 ** (-8.0 * np.arange(1, n + 1) / n), dtype=np.float32)


def _params(*sem):
    return pltpu.CompilerParams(dimension_semantics=sem, vmem_limit_bytes=VMEM_LIMIT)


def _rms(x, g):
    return x * lax.rsqrt(jnp.mean(x * x, axis=-1, keepdims=True) + RMS_EPS) * g


def _dot(a, b):
    return jnp.dot(a, b, preferred_element_type=F32)


def _residue_spec(seq, dil, width):
    tiles_per_seq = seq // TOKEN_TILE
    return pl.BlockSpec((1, dil, TOKEN_TILE // dil, width),
                        lambda i: (i // tiles_per_seq, 0, i % tiles_per_seq, 0))


def _proj_kernel(x_ref, g_ref, w_ref, *refs):
    pa_refs, (qb_ref, kv_ref, scr_ref, tmp_ref) = refs[:len(A_GROUPS)], refs[len(A_GROUPS):]
    h = _rms(x_ref[...], g_ref[...]).astype(BF16)
    for gi, (_, dil) in enumerate(A_GROUPS):
        cols = slice(gi * 3 * A_WIDTH, (gi + 1) * 3 * A_WIDTH)
        res = _dot(h, w_ref[:, cols])
        if dil == 1:
            pa_refs[gi][0, 0] = res.astype(BF16)
        else:
            chunks = res.shape[1] // LANES
            for c in range(chunks):
                scr_ref[c] = res[:, c * LANES:(c + 1) * LANES]
            rows = TOKEN_TILE // dil
            if dil <= ROW_STRIDE:
                src = lambda c, r: scr_ref[c, pl.ds(r, rows, stride=dil), :]
            else:
                outer = dil // ROW_STRIDE
                for c in range(chunks):
                    for r1 in range(ROW_STRIDE):
                        tmp_ref[c, r1] = scr_ref[c, pl.ds(r1, TOKEN_TILE // ROW_STRIDE, stride=ROW_STRIDE), :]
                src = lambda c, r: tmp_ref[c, r % ROW_STRIDE, pl.ds(r // ROW_STRIDE, rows, stride=outer), :]
            for r in range(dil):
                parts = [src(c, r) for c in range(chunks)]
                pa_refs[gi][0, r] = jnp.concatenate(parts, axis=-1).astype(BF16)
    qb_ref[...] = _dot(h, w_ref[:, A_COLS:A_COLS + B_Q_WIDTH]).astype(BF16)
    kv_ref[...] = _dot(h, w_ref[:, A_COLS + B_Q_WIDTH:]).astype(BF16)


def _proj(x, g, w_in, bsz, seq):
    t = x.shape[0]
    tm = TOKEN_TILE
    row = lambda i: (i, 0)
    fixed = lambda i: (0, 0)
    gw = 3 * A_WIDTH
    return pl.pallas_call(
        _proj_kernel,
        grid=(t // tm,),
        in_specs=[pl.BlockSpec((tm, D_MODEL), row), pl.BlockSpec((1, D_MODEL), fixed),
                  pl.BlockSpec((D_MODEL, N_IN), fixed)],
        out_specs=[_residue_spec(seq, dil, gw) for _, dil in A_GROUPS]
                  + [pl.BlockSpec((tm, B_Q_WIDTH), row), pl.BlockSpec((tm, 2 * B_KV_WIDTH), row)],
        out_shape=[jax.ShapeDtypeStruct((bsz, dil, seq // dil, gw), BF16) for _, dil in A_GROUPS]
                  + [jax.ShapeDtypeStruct((t, B_Q_WIDTH), BF16), jax.ShapeDtypeStruct((t, 2 * B_KV_WIDTH), BF16)],
        scratch_shapes=[pltpu.VMEM((gw // LANES, tm, LANES), F32),
                        pltpu.VMEM((gw // LANES, ROW_STRIDE, tm // ROW_STRIDE, LANES), F32)],
        compiler_params=_params("parallel"),
        name="proj",
    )(x, g, w_in)


BAND_CASES = 3


def _band(q0, n, kw, half_window):
    start = pl.multiple_of(jnp.clip(q0 - half_window, 0, n - kw), half_window)
    return start, (q0 - start) // half_window


def _fill_bias(bias_ref, slopes, tq, kw, half_window):
    rel = lax.broadcasted_iota(jnp.int32, (tq, kw), 0) - lax.broadcasted_iota(jnp.int32, (tq, kw), 1)
    for case in range(BAND_CASES):
        dist = jnp.abs(rel + case * half_window)
        for h, slope in enumerate(slopes):
            bias_ref[case, h] = jnp.where(dist <= half_window, -slope * dist.astype(F32), NEG_INF)


Q_SCALE = HEAD_DIM ** -0.5


def _head_scores(qh, kh, bias):
    return lax.dot_general(qh, kh, (((1,), (1,)), ((), ())), preferred_element_type=F32) + bias


def _attn_a_shapes(n, dil):
    tq = min(ATTN_Q_TILE, n)
    res = max(1, min(dil, ATTN_A_ROWS // n))
    return tq, min(n, tq + 2 * A_HALF_WINDOW), res, min(ATTN_A_BLOCKS, res * (n // tq))


def _attn_a_kernel(x_ref, o_ref, lse_ref, s_ref, p_ref, bias_ref, *, n, dil, slopes):
    tq, kw, res, nb = _attn_a_shapes(n, dil)
    per_res = n // tq
    lane = lax.broadcasted_iota(jnp.int32, (tq, LANES), 1)
    lane_head = lax.broadcasted_iota(jnp.int32, (tq, A_WIDTH), 1) // HEAD_DIM
    n_pairs = nb * A_HEADS

    @pl.when((pl.program_id(0) == 0) & (pl.program_id(1) == 0))
    def _():
        _fill_bias(bias_ref, slopes, tq, kw, A_HALF_WINDOW)

    def step(it, carry):
        blocks = []
        for j in range(nb):
            blk = it * nb + j
            r = blk // per_res
            q0 = pl.multiple_of((blk % per_res) * tq, tq)
            start, case = _band(q0, n, kw, A_HALF_WINDOW)
            blocks.append((r, q0, case, x_ref[0, r, pl.ds(q0, tq), 0:A_WIDTH] * Q_SCALE,
                           x_ref[0, r, pl.ds(start, kw), A_WIDTH:2 * A_WIDTH],
                           x_ref[0, r, pl.ds(start, kw), 2 * A_WIDTH:3 * A_WIDTH]))
        for j, (_, _, case, q, k, _) in enumerate(blocks):
            q_heads = jnp.concatenate([jnp.where(lane_head == h, q, jnp.zeros_like(q)) for h in range(A_HEADS)],
                                      axis=0)
            s_all = lax.dot_general(q_heads, k, (((1,), (1,)), ((), ())), preferred_element_type=F32)
            for h in range(A_HEADS):
                s_ref[j * A_HEADS + h] = s_all[h * tq:(h + 1) * tq] + bias_ref[case, h]
        ms = [jnp.max(s_ref[i], axis=-1, keepdims=True) for i in range(n_pairs)]
        ls = []
        for i in range(n_pairs):
            p = jnp.exp(s_ref[i] - ms[i])
            ls.append(jnp.sum(p, axis=-1, keepdims=True))
            p_ref[i] = p.astype(BF16)
        for j in range(nb):
            r, q0, v = blocks[j][0], blocks[j][1], blocks[j][5]
            pv = _dot(p_ref[pl.ds(j * A_HEADS, A_HEADS)].reshape(A_HEADS * tq, kw), v)
            out = jnp.zeros((tq, A_WIDTH), F32)
            lse_tile = jnp.zeros((tq, LANES), F32)
            for h in range(A_HEADS):
                i = j * A_HEADS + h
                out = jnp.where(lane_head == h, pv[h * tq:(h + 1) * tq] / ls[i], out)
                lse_tile = jnp.where(lane == h, ms[i] + jnp.log(ls[i]), lse_tile)
            o_ref[0, r, pl.ds(q0, tq), :] = out.astype(BF16)
            lse_ref[0, r, pl.ds(q0, tq), :] = lse_tile
        return carry

    lax.fori_loop(0, res * per_res // nb, step, 0)


def _attn_a(pa, gi):
    window, dil = A_GROUPS[gi]
    assert window // (2 * dil) == A_HALF_WINDOW
    bsz, _, n, gw = pa.shape
    tq, kw, res, nb = _attn_a_shapes(n, dil)
    assert dil % res == 0 and (res * (n // tq)) % nb == 0
    slopes = _alibi_slopes(len(A_GROUPS) * A_HEADS).reshape(len(A_GROUPS), A_HEADS)[gi]
    slopes = tuple(float(s * np.float32(dil)) for s in slopes)
    block = lambda w: pl.BlockSpec((1, res, n, w), lambda b, r: (b, r, 0, 0))
    return pl.pallas_call(
        functools.partial(_attn_a_kernel, n=n, dil=dil, slopes=slopes),
        grid=(bsz, dil // res),
        in_specs=[block(gw)],
        out_specs=[block(A_WIDTH), block(LANES)],
        out_shape=[jax.ShapeDtypeStruct((bsz, dil, n, A_WIDTH), BF16),
                   jax.ShapeDtypeStruct((bsz, dil, n, LANES), F32)],
        scratch_shapes=[pltpu.VMEM((nb * A_HEADS, tq, kw), F32), pltpu.VMEM((nb * A_HEADS, tq, kw), BF16),
                        pltpu.VMEM((BAND_CASES, A_HEADS, tq, kw), F32)],
        compiler_params=_params("arbitrary", "arbitrary"),
        name=f"attn_a{gi}",
    )(pa)


def _attn_b_kernel(sink_ref, q_ref, kv_ref, o_ref, s_ref, p_ref, bias_ref, *, n, slopes):
    hw = B_HALF_WINDOW
    tq = min(ATTN_Q_TILE, n)
    kw = min(n, tq + 2 * hw)
    heads = range(B_Q_HEADS)
    low = lax.broadcasted_iota(jnp.int32, (tq, LANES), 1) < HEAD_DIM
    swap = lambda t: jnp.concatenate([t[:, HEAD_DIM:], t[:, :HEAD_DIM]], axis=1)

    @pl.when(pl.program_id(0) == 0)
    def _():
        _fill_bias(bias_ref, slopes, tq, kw, hw)

    nb = min(ATTN_B_BLOCKS, n // tq)
    pairs = [(j, hq) for j in range(nb) for hq in heads]

    def step(it, carry):
        blocks = []
        for j in range(nb):
            q0 = pl.multiple_of((it * nb + j) * tq, tq)
            start, case = _band(q0, n, kw, hw)
            blocks.append((q0, case, q_ref[0, pl.ds(q0, tq), :] * Q_SCALE,
                           kv_ref[0, pl.ds(start, kw), 0:B_KV_WIDTH],
                           kv_ref[0, pl.ds(start, kw), B_KV_WIDTH:2 * B_KV_WIDTH]))
        for j, (_, case, q, k2, _) in enumerate(blocks):
            for hk in range(B_KV_HEADS):
                rows = []
                for hq in range(hk * B_GROUP, (hk + 1) * B_GROUP):
                    tile = q[:, (hq // 2) * LANES:(hq // 2 + 1) * LANES]
                    moved = tile if hq % 2 == hk else swap(tile)
                    rows.append(jnp.where(low == (hk == 0), moved, jnp.zeros_like(moved)))
                s_all = lax.dot_general(jnp.concatenate(rows, axis=0), k2, (((1,), (1,)), ((), ())),
                                        preferred_element_type=F32)
                for g in range(B_GROUP):
                    hq = hk * B_GROUP + g
                    s_ref[j * B_Q_HEADS + hq] = s_all[g * tq:(g + 1) * tq] + bias_ref[case, hq]
        ms = [jnp.maximum(jnp.max(s_ref[i], axis=-1, keepdims=True), sink_ref[hq]) for i, (_, hq) in enumerate(pairs)]
        ls = []
        for i, (_, hq) in enumerate(pairs):
            p = jnp.exp(s_ref[i] - ms[i])
            ls.append(jnp.sum(p, axis=-1, keepdims=True) + jnp.exp(sink_ref[hq] - ms[i]))
            p_ref[i] = p.astype(BF16)
        for j, (q0, _, _, _, v2) in enumerate(blocks):
            placed = []
            for hk in range(B_KV_HEADS):
                first = j * B_Q_HEADS + hk * B_GROUP
                pv = _dot(p_ref[pl.ds(first, B_GROUP)].reshape(B_GROUP * tq, kw), v2)
                for g in range(B_GROUP):
                    hq = hk * B_GROUP + g
                    o = pv[g * tq:(g + 1) * tq] / ls[first + g]
                    placed.append(o if hq % 2 == hk else swap(o))
            tiles = [jnp.where(low, placed[t], placed[t + 1]) for t in range(0, B_Q_HEADS, 2)]
            o_ref[0, pl.ds(q0, tq), :] = jnp.concatenate(tiles, axis=1).astype(BF16)
        return carry

    lax.fori_loop(0, n // (tq * nb), step, 0)


def _attn_b(qb, kvb, sink, bsz, seq):
    slopes = tuple(float(s) for s in _alibi_slopes(B_Q_HEADS))
    tq = min(ATTN_Q_TILE, seq)
    kw = min(seq, tq + 2 * B_HALF_WINDOW)
    nb = min(ATTN_B_BLOCKS, seq // tq)
    assert seq % (tq * nb) == 0
    out = pl.pallas_call(
        functools.partial(_attn_b_kernel, n=seq, slopes=slopes),
        grid=(bsz,),
        in_specs=[pl.BlockSpec(memory_space=pltpu.SMEM),
                  pl.BlockSpec((1, seq, B_Q_WIDTH), lambda b: (b, 0, 0)),
                  pl.BlockSpec((1, seq, 2 * B_KV_WIDTH), lambda b: (b, 0, 0))],
        out_specs=pl.BlockSpec((1, seq, B_Q_WIDTH), lambda b: (b, 0, 0)),
        out_shape=jax.ShapeDtypeStruct((bsz, seq, B_Q_WIDTH), BF16),
        scratch_shapes=[pltpu.VMEM((nb * B_Q_HEADS, tq, kw), F32), pltpu.VMEM((nb * B_Q_HEADS, tq, kw), BF16),
                        pltpu.VMEM((BAND_CASES, B_Q_HEADS, tq, kw), F32)],
        compiler_params=_params("arbitrary"),
        name="attn_b",
    )(sink, qb.reshape(bsz, seq, B_Q_WIDTH), kvb.reshape(bsz, seq, 2 * B_KV_WIDTH))
    return out.reshape(bsz * seq, B_Q_WIDTH)


def _split_bf16(a):
    hi = a.astype(BF16)
    return hi, (a - hi.astype(F32)).astype(BF16)


def _top2(h2, wr_a, wr_b):
    tm = h2.shape[0]
    nt = (((1,), (1,)), ((), ()))
    h_hi, h_lo = _split_bf16(h2)
    by_hi = lax.dot_general(wr_a, h_hi, nt, preferred_element_type=F32)
    by_lo = lax.dot_general(wr_b, h_lo, nt, preferred_element_type=F32)
    logits = by_hi[:N_EXPERTS] + (by_hi[N_EXPERTS:] + by_lo[:N_EXPERTS])
    row = lax.broadcasted_iota(jnp.int32, logits.shape, 0)
    m1 = jnp.max(logits, axis=0, keepdims=True)
    i1 = jnp.min(jnp.where(logits == m1, row, N_EXPERTS), axis=0, keepdims=True)
    l2 = jnp.where(row == i1, -jnp.inf, logits)
    m2 = jnp.max(l2, axis=0, keepdims=True)
    i2 = jnp.min(jnp.where(l2 == m2, row, N_EXPERTS), axis=0, keepdims=True)
    e2 = jnp.exp(m2 - m1)
    den = 1.0 + e2
    packed = jnp.where(row == 0, i1.astype(F32),
                       jnp.where(row == 1, i2.astype(F32),
                                 jnp.where(row == 2, 1.0 / den, jnp.where(row == 3, e2 / den, 0.0))))
    padded = jnp.concatenate([packed, jnp.zeros((LANES - N_EXPERTS, tm), F32)], axis=0)
    return padded.T


def _interleave_rows(dst_ref, tmp_ref, piece, dil):
    rows = TOKEN_TILE // dil
    if dil <= ROW_STRIDE:
        for r in range(dil):
            dst_ref[pl.ds(r, rows, stride=dil), :] = piece(r)
        return
    outer = dil // ROW_STRIDE
    for r in range(dil):
        tmp_ref[r % ROW_STRIDE, pl.ds(r // ROW_STRIDE, rows, stride=outer), :] = piece(r)
    for r1 in range(ROW_STRIDE):
        dst_ref[pl.ds(r1, TOKEN_TILE // ROW_STRIDE, stride=ROW_STRIDE), :] = tmp_ref[r1]


def _merge_kernel(*refs, moe):
    ng = len(A_GROUPS)
    x_ref = refs[0]
    o_refs, l_refs = refs[1:1 + ng], refs[1 + ng:1 + 2 * ng]
    yb_ref, gmix_ref, wgate_ref, bgate_ref, wpa_ref, wpb_ref, wout_ref, gffn_ref = refs[1 + 2 * ng:9 + 2 * ng]
    rest = refs[9 + 2 * ng:]
    if moe:
        wr_hi_ref, wr_lo_ref, xo_ref, h2_ref, rout_ref, so_ref, sl_ref, tmp_ref = rest
    else:
        xo_ref, h2_ref, so_ref, sl_ref, tmp_ref = rest
    x = x_ref[...]
    h = _rms(x, gmix_ref[...]).astype(BF16)
    z_b = _dot(yb_ref[...], wpb_ref[...])
    g_a = jax.nn.sigmoid(_dot(h, wgate_ref[:, :D_MODEL]) + bgate_ref[:, :D_MODEL])
    g_b = jax.nn.sigmoid(_dot(h, wgate_ref[:, D_MODEL:]) + bgate_ref[:, D_MODEL:])
    outs, lses = [], []
    for gi, (_, dil) in enumerate(A_GROUPS):
        if dil == 1:
            outs.append(o_refs[gi][0, 0].astype(F32))
            lses.append(l_refs[gi][0, 0])
        else:
            chunks = A_WIDTH // LANES
            for c in range(chunks):
                piece = lambda r, c=c, gi=gi: o_refs[gi][0, r, :, c * LANES:(c + 1) * LANES].astype(F32)
                _interleave_rows(so_ref.at[gi, c], tmp_ref.at[c], piece, dil)
            _interleave_rows(sl_ref.at[gi], tmp_ref.at[chunks], lambda r, gi=gi: l_refs[gi][0, r], dil)
            outs.append(jnp.concatenate([so_ref[gi, c] for c in range(chunks)], axis=-1))
            lses.append(sl_ref[gi])
    mx = jnp.maximum(jnp.maximum(lses[0], lses[1]), lses[2])
    es = [jnp.exp(l - mx) for l in lses]
    den = es[0] + es[1] + es[2]
    heads = []
    for hh in range(A_HEADS):
        cols = slice(hh * HEAD_DIM, (hh + 1) * HEAD_DIM)
        acc = None
        for g in range(ng):
            term = (es[g][:, hh:hh + 1] / den[:, hh:hh + 1]) * outs[g][:, cols]
            acc = term if acc is None else acc + term
        heads.append(acc)
    y_a = jnp.concatenate(heads, axis=-1).astype(BF16)
    merged = g_a * _dot(y_a, wpa_ref[...]) + g_b * z_b
    xn = x + _dot(merged.astype(BF16), wout_ref[...])
    xo_ref[...] = xn
    h2 = _rms(xn, gffn_ref[...])
    h2_ref[...] = h2.astype(h2_ref.dtype)
    if moe:
        rout_ref[...] = _top2(h2, wr_hi_ref[...], wr_lo_ref[...])


def _merge(x, outs, lses, y_b, seq, gmix, w_gate, b_gate, w_pa, w_pb, w_out, gffn, w_router=None):
    t = x.shape[0]
    tm = TOKEN_TILE
    moe = w_router is not None
    row = lambda i: (i, 0)
    fixed = lambda i: (0, 0)
    full = lambda a: pl.BlockSpec(a.shape, fixed)
    weights = [gmix, w_gate, b_gate, w_pa, w_pb, w_out, gffn]
    out_specs = [pl.BlockSpec((tm, D_MODEL), row), pl.BlockSpec((tm, D_MODEL), row)]
    out_shape = [jax.ShapeDtypeStruct((t, D_MODEL), F32), jax.ShapeDtypeStruct((t, D_MODEL), F32 if moe else BF16)]
    if moe:
        wr_hi, wr_lo = _split_bf16(w_router.T)
        weights.extend([jnp.concatenate([wr_hi, wr_lo]), jnp.concatenate([wr_hi, jnp.zeros_like(wr_hi)])])
        out_specs.append(pl.BlockSpec((tm, LANES), row))
        out_shape.append(jax.ShapeDtypeStruct((t, LANES), F32))
    ng = len(A_GROUPS)
    return pl.pallas_call(
        functools.partial(_merge_kernel, moe=moe),
        grid=(t // tm,),
        in_specs=[pl.BlockSpec((tm, D_MODEL), row)]
                 + [_residue_spec(seq, dil, A_WIDTH) for _, dil in A_GROUPS]
                 + [_residue_spec(seq, dil, LANES) for _, dil in A_GROUPS]
                 + [pl.BlockSpec((tm, B_Q_WIDTH), row)] + [full(a) for a in weights],
        out_specs=out_specs,
        out_shape=out_shape,
        scratch_shapes=[pltpu.VMEM((ng, A_WIDTH // LANES, tm, LANES), F32), pltpu.VMEM((ng, tm, LANES), F32),
                        pltpu.VMEM((A_WIDTH // LANES + 1, ROW_STRIDE, tm // ROW_STRIDE, LANES), F32)],
        compiler_params=_params("parallel"),
        name="merge",
    )(x, *outs, *lses, y_b, *weights)


def _swiglu_step(xb, wg, wu, wd):
    a = _dot(xb, wg)
    return _dot((jax.nn.silu(a) * _dot(xb, wu)).astype(BF16), wd)


def _ffn_kernel(h_ref, x_ref, wg_ref, wu_ref, wd_ref, o_ref, hm_ref):
    h = h_ref[...]
    d_ff = wg_ref.shape[1]
    for lo in range(0, d_ff, FF_CHUNK_DENSE):
        cols = slice(lo, min(lo + FF_CHUNK_DENSE, d_ff))
        hm_ref[:, cols] = (jax.nn.silu(_dot(h, wg_ref[:, cols])) * _dot(h, wu_ref[:, cols])).astype(BF16)
    o_ref[...] = x_ref[...] + _dot(hm_ref[...], wd_ref[...])


def _ffn(h2, x, w_g, w_u, w_d):
    t = x.shape[0]
    tm = TOKEN_TILE
    d_ff = w_g.shape[1]
    row = lambda i: (i, 0)
    fixed = lambda i: (0, 0)
    return pl.pallas_call(
        _ffn_kernel,
        grid=(t // tm,),
        in_specs=[pl.BlockSpec((tm, D_MODEL), row), pl.BlockSpec((tm, D_MODEL), row),
                  pl.BlockSpec((D_MODEL, d_ff), fixed), pl.BlockSpec((D_MODEL, d_ff), fixed),
                  pl.BlockSpec((d_ff, D_MODEL), fixed)],
        out_specs=pl.BlockSpec((tm, D_MODEL), row),
        out_shape=jax.ShapeDtypeStruct((t, D_MODEL), F32),
        scratch_shapes=[pltpu.VMEM((tm, d_ff), BF16)],
        compiler_params=_params("parallel"),
        name="ffn_dense",
    )(h2, x, w_g, w_u, w_d)


def _dispatch_kernel(zero_ref, dest_ref, h_ref, xs_ref, zbuf_ref, sem, zsem):
    tm = h_ref.shape[0]

    @pl.when(pl.program_id(0) == 0)
    def _():
        zbuf_ref[...] = jnp.zeros_like(zbuf_ref)
        n_zero = zero_ref.shape[0]
        zero_copy = lambda z: pltpu.make_async_copy(
            zbuf_ref, xs_ref.at[pl.ds(pl.multiple_of(zero_ref[z], MOE_TILE), MOE_TILE), :], zsem)
        for z in range(n_zero):
            zero_copy(z).start()
            zero_copy(z).wait()

    def body(r, c):
        for k in range(TOP_K):
            d = dest_ref[0, 0, TOP_K * r + k]
            pltpu.make_async_copy(h_ref.at[pl.ds(r, 1), :], xs_ref.at[pl.ds(d, 1), :], sem).start(priority=k)
        return c

    lax.fori_loop(0, tm, body, 0, unroll=ROW_LOOP_UNROLL)
    for k in range(TOP_K):
        pltpu.make_async_copy(h_ref, xs_ref.at[pl.ds(0, tm), :], sem).wait()


def _dispatch(h2, dest, zero_blocks, cap):
    t = h2.shape[0]
    tm = DISPATCH_TILE
    return pl.pallas_call(
        _dispatch_kernel,
        grid_spec=pltpu.PrefetchScalarGridSpec(
            num_scalar_prefetch=1,
            grid=(t // tm,),
            in_specs=[pl.BlockSpec((1, 1, TOP_K * tm), lambda i, z: (i, 0, 0), memory_space=pltpu.SMEM),
                      pl.BlockSpec((tm, D_MODEL), lambda i, z: (i, 0))],
            out_specs=pl.BlockSpec(memory_space=pl.ANY),
            scratch_shapes=[pltpu.VMEM((MOE_TILE, D_MODEL), F32), pltpu.SemaphoreType.DMA(()),
                            pltpu.SemaphoreType.DMA(())],
        ),
        out_shape=jax.ShapeDtypeStruct((cap, D_MODEL), F32),
        compiler_params=_params("arbitrary"),
        name="dispatch",
    )(zero_blocks, dest.reshape(t // tm, 1, TOP_K * tm), h2)


def _expert_kernel(be_ref, used_ref, xs_ref, wg_ref, wu_ref, wd_ref, ys_ref, xb_ref, acc_ref):
    del be_ref
    i, j = pl.program_id(0), pl.program_id(1)
    last = pl.num_programs(1) - 1
    active = i < used_ref[0]

    @pl.when(active & (j == 0))
    def _():
        xb_ref[...] = xs_ref[...].astype(BF16)

    @pl.when(active)
    def _():
        part = _swiglu_step(xb_ref[...], wg_ref[0], wu_ref[0], wd_ref[0])

        @pl.when(j == 0)
        def _():
            acc_ref[...] = part

        @pl.when(j > 0)
        def _():
            acc_ref[...] += part

        @pl.when(j == last)
        def _():
            ys_ref[...] = acc_ref[...]

    @pl.when(jnp.logical_not(active) & (j == last))
    def _():
        ys_ref[...] = jnp.zeros_like(ys_ref)


def _experts(xs, block_e, n_used, w_g, w_u, w_d):
    cap = xs.shape[0]
    tm = MOE_TILE
    d_ff = w_g.shape[2]
    tf = d_ff // FF_SPLIT_EXPERT
    assert tf * FF_SPLIT_EXPERT == d_ff and tf % LANES == 0
    chunk = lambda i, j, used: jnp.where(i < used[0], j, FF_SPLIT_EXPERT - 1)
    return pl.pallas_call(
        _expert_kernel,
        grid_spec=pltpu.PrefetchScalarGridSpec(
            num_scalar_prefetch=2,
            grid=(cap // tm, FF_SPLIT_EXPERT),
            in_specs=[pl.BlockSpec((tm, D_MODEL), lambda i, j, be, used: (i, 0)),
                      pl.BlockSpec((1, D_MODEL, tf), lambda i, j, be, used: (be[i], 0, chunk(i, j, used))),
                      pl.BlockSpec((1, D_MODEL, tf), lambda i, j, be, used: (be[i], 0, chunk(i, j, used))),
                      pl.BlockSpec((1, tf, D_MODEL), lambda i, j, be, used: (be[i], chunk(i, j, used), 0))],
            out_specs=pl.BlockSpec((tm, D_MODEL), lambda i, j, be, used: (i, 0)),
            scratch_shapes=[pltpu.VMEM((tm, D_MODEL), BF16), pltpu.VMEM((tm, D_MODEL), F32)],
        ),
        out_shape=jax.ShapeDtypeStruct((cap, D_MODEL), F32),
        compiler_params=_params("arbitrary", "arbitrary"),
        name="experts",
    )(block_e, n_used, xs, w_g, w_u, w_d)


def _combine_kernel(dest_ref, dest_next_ref, r_ref, x_ref, ys_ref, g_ref, o_ref, y_ref, sem):
    tm = x_ref.shape[0]
    i = pl.program_id(0)

    def fetch(idx_ref, slot):
        for r in range(tm):
            for k in range(TOP_K):
                d = idx_ref[0, 0, TOP_K * r + k]
                pltpu.make_async_copy(ys_ref.at[pl.ds(d, 1), :], y_ref.at[slot, k, pl.ds(r, 1), :],
                                      sem.at[slot, k]).start(priority=k)

    @pl.when(i == 0)
    def _():
        fetch(dest_ref, 0)

    for parity in range(2):
        @pl.when((i + 1 < pl.num_programs(0)) & ((i + 1) % 2 == parity))
        def _():
            fetch(dest_next_ref, parity)

    slot = i % 2
    for k in range(TOP_K):
        pltpu.make_async_copy(ys_ref.at[pl.ds(0, tm), :], y_ref.at[slot, k], sem.at[slot, k]).wait()
    gates = r_ref[...]
    moe = gates[:, 2:3] * y_ref[slot, 0] + gates[:, 3:4] * y_ref[slot, 1]
    o_ref[...] = _rms(x_ref[...] + moe, g_ref[...])


def _combine(ys, dest, rout, x, g_final):
    t = x.shape[0]
    tm = COMBINE_TILE
    steps = t // tm
    dest3 = dest.reshape(steps, 1, TOP_K * tm)
    return pl.pallas_call(
        _combine_kernel,
        grid=(steps,),
        in_specs=[pl.BlockSpec((1, 1, TOP_K * tm), lambda i: (i, 0, 0), memory_space=pltpu.SMEM),
                  pl.BlockSpec((1, 1, TOP_K * tm), lambda i: (jnp.minimum(i + 1, steps - 1), 0, 0),
                               memory_space=pltpu.SMEM),
                  pl.BlockSpec((tm, LANES), lambda i: (i, 0)),
                  pl.BlockSpec((tm, D_MODEL), lambda i: (i, 0)),
                  pl.BlockSpec(memory_space=pl.ANY),
                  pl.BlockSpec((1, D_MODEL), lambda i: (0, 0))],
        out_specs=pl.BlockSpec((tm, D_MODEL), lambda i: (i, 0)),
        out_shape=jax.ShapeDtypeStruct((t, D_MODEL), F32),
        scratch_shapes=[pltpu.VMEM((2, TOP_K, tm, D_MODEL), F32), pltpu.SemaphoreType.DMA((2, TOP_K))],
        compiler_params=_params("arbitrary"),
        name="combine",
    )(dest3, dest3, rout, x, ys, g_final)


def _routing_tables(rout, t):
    tm = MOE_TILE
    e_flat = rout[:, :TOP_K].astype(jnp.int32).reshape(-1)
    onehot = (e_flat[:, None] == jnp.arange(N_EXPERTS, dtype=jnp.int32)[None, :]).astype(jnp.int32)
    csum = jnp.cumsum(onehot, axis=0)
    counts = csum[-1]
    padded = (counts + tm - 1) // tm * tm
    pend = jnp.cumsum(padded)
    pstart = pend - padded
    dest = jnp.sum(onehot * (csum - 1 + pstart[None, :]), axis=1).astype(jnp.int32)
    n_blocks = -(-(t * TOP_K) // tm) + N_EXPERTS
    block_start = jnp.arange(n_blocks, dtype=jnp.int32) * tm
    block_e = jnp.minimum(jnp.sum((pend[None, :] <= block_start[:, None]).astype(jnp.int32), axis=1), N_EXPERTS - 1)
    zero_blocks = jnp.concatenate([jnp.maximum(pend - tm, 0), block_start[n_blocks - N_EXPERTS:]]).astype(jnp.int32)
    n_used = (pend[N_EXPERTS - 1:] // tm).astype(jnp.int32)
    return dest, block_e, n_used, zero_blocks, n_blocks * tm


def _trunk(x3, wts):
    bsz, seq, _ = x3.shape
    assert seq % TOKEN_TILE == 0
    x = x3.reshape(bsz * seq, D_MODEL)
    depth = len(wts["layers"])
    for li, lw in enumerate(wts["layers"]):
        *pas, qb, kvb = _proj(x, lw["norm_mix_g"], lw["w_in"], bsz, seq)
        outs, lses = zip(*[_attn_a(pa, gi) for gi, pa in enumerate(pas)])
        y_b = _attn_b(qb, kvb, lw["sink"], bsz, seq)
        mixer = (lw["norm_mix_g"], lw["w_gate"], lw["b_gate"], lw["w_proj_a"], lw["w_proj_b"], lw["w_out"],
                 lw["norm_ffn_g"])
        if "w_router" not in lw:
            assert li < depth - 1
            x, h2 = _merge(x, outs, lses, y_b, seq, *mixer)
            x = _ffn(h2, x, lw["w_ff_gate"], lw["w_ff_up"], lw["w_ff_down"])
        else:
            assert li == depth - 1
            x, h2, rout = _merge(x, outs, lses, y_b, seq, *mixer, w_router=lw["w_router"])
            dest, block_e, n_used, zero_blocks, cap = _routing_tables(rout, bsz * seq)
            xs = _dispatch(h2, dest, zero_blocks, cap)
            ys = _experts(xs, block_e, n_used, lw["w_e_gate"], lw["w_e_up"], lw["w_e_down"])
            x = _combine(ys, dest, rout, x, wts["norm_final_g"])
    return x.reshape(bsz, seq, D_MODEL)


def kernel(x_prompt, x_sample, norm_mix_g, w_in, w_gate, b_gate, w_proj_a, w_proj_b, w_out, sink, norm_ffn_g,
           w_ff_gate, w_ff_up, w_ff_down, w_router, w_e_gate, w_e_up, w_e_down, norm_final_g):
    depth = w_in.shape[0]
    layers = []
    for li in range(depth):
        lw = {
            "norm_mix_g": norm_mix_g[li].reshape(1, D_MODEL), "w_in": w_in[li].astype(BF16),
            "w_gate": w_gate[li].astype(BF16), "b_gate": b_gate[li].reshape(1, 2 * D_MODEL),
            "w_proj_a": w_proj_a[li].astype(BF16), "w_proj_b": w_proj_b[li].astype(BF16),
            "w_out": w_out[li].astype(BF16), "sink": sink[li], "norm_ffn_g": norm_ffn_g[li].reshape(1, D_MODEL),
        }
        i = li // 2
        if li % 2 == 0:
            lw.update(w_ff_gate=w_ff_gate[i].astype(BF16), w_ff_up=w_ff_up[i].astype(BF16),
                      w_ff_down=w_ff_down[i].astype(BF16))
        else:
            lw.update(w_router=w_router[i], w_e_gate=w_e_gate[i].astype(BF16), w_e_up=w_e_up[i].astype(BF16),
                      w_e_down=w_e_down[i].astype(BF16))
        layers.append(lw)
    wts = {"layers": layers, "norm_final_g": norm_final_g.reshape(1, D_MODEL)}
    return _trunk(x_prompt, wts), _trunk(x_sample, wts)
```

```python
import functools

import numpy as np
import jax
import jax.numpy as jnp
from jax import lax
from jax.experimental import pallas as pl
from jax.experimental.pallas import tpu as pltpu

F32 = jnp.float32
BF16 = jnp.bfloat16

D_MODEL = 1024
HEAD_DIM = 64
A_GROUPS = ((128, 1), (512, 4), (2048, 16))
A_HALF_WINDOW = 64
A_HEADS = 4
A_WIDTH = A_HEADS * HEAD_DIM
A_COLS = len(A_GROUPS) * 3 * A_WIDTH
B_Q_HEADS = 8
B_KV_HEADS = 2
B_GROUP = B_Q_HEADS // B_KV_HEADS
B_Q_WIDTH = B_Q_HEADS * HEAD_DIM
B_KV_WIDTH = B_KV_HEADS * HEAD_DIM
B_HALF_WINDOW = 128
N_IN = A_COLS + B_Q_WIDTH + 2 * B_KV_WIDTH
N_EXPERTS = 8
TOP_K = 2
RMS_EPS = 1e-6
NEG_INF = -1e30

LANES = 128
TOKEN_TILE = 512
ATTN_Q_TILE = 128
ATTN_A_ROWS = 2048
ATTN_A_BLOCKS = 4
ATTN_B_BLOCKS = 2
FF_CHUNK_DENSE = 1536
FF_SPLIT_EXPERT = 2
MOE_TILE = 512
DISPATCH_TILE = 512
COMBINE_TILE = 512
ROW_STRIDE = 4
VMEM_LIMIT = 52 * 1024 * 1024


def _alibi_slopes(n):
    return np.asarray(2.0 ** (-8.0 * np.arange(1, n + 1) / n), dtype=np.float32)


def _params(*sem):
    return pltpu.CompilerParams(dimension_semantics=sem, vmem_limit_bytes=VMEM_LIMIT)


def _rms(x, g):
    return x * lax.rsqrt(jnp.mean(x * x, axis=-1, keepdims=True) + RMS_EPS) * g


def _dot(a, b):
    return jnp.dot(a, b, preferred_element_type=F32)


def _residue_spec(seq, dil, width):
    tiles_per_seq = seq // TOKEN_TILE
    return pl.BlockSpec((1, dil, TOKEN_TILE // dil, width),
                        lambda i: (i // tiles_per_seq, 0, i % tiles_per_seq, 0))


def _proj_kernel(x_ref, g_ref, w_ref, *refs):
    pa_refs, (qb_ref, kv_ref, scr_ref, tmp_ref) = refs[:len(A_GROUPS)], refs[len(A_GROUPS):]
    h = _rms(x_ref[...], g_ref[...]).astype(BF16)
    for gi, (_, dil) in enumerate(A_GROUPS):
        cols = slice(gi * 3 * A_WIDTH, (gi + 1) * 3 * A_WIDTH)
        res = _dot(h, w_ref[:, cols])
        if dil == 1:
            pa_refs[gi][0, 0] = res.astype(BF16)
        else:
            chunks = res.shape[1] // LANES
            for c in range(chunks):
                scr_ref[c] = res[:, c * LANES:(c + 1) * LANES]
            rows = TOKEN_TILE // dil
            if dil <= ROW_STRIDE:
                src = lambda c, r: scr_ref[c, pl.ds(r, rows, stride=dil), :]
            else:
                outer = dil // ROW_STRIDE
                for c in range(chunks):
                    for r1 in range(ROW_STRIDE):
                        tmp_ref[c, r1] = scr_ref[c, pl.ds(r1, TOKEN_TILE // ROW_STRIDE, stride=ROW_STRIDE), :]
                src = lambda c, r: tmp_ref[c, r % ROW_STRIDE, pl.ds(r // ROW_STRIDE, rows, stride=outer), :]
            for r in range(dil):
                parts = [src(c, r) for c in range(chunks)]
                pa_refs[gi][0, r] = jnp.concatenate(parts, axis=-1).astype(BF16)
    qb_ref[...] = _dot(h, w_ref[:, A_COLS:A_COLS + B_Q_WIDTH]).astype(BF16)
    kv_ref[...] = _dot(h, w_ref[:, A_COLS + B_Q_WIDTH:]).astype(BF16)


def _proj(x, g, w_in, bsz, seq):
    t = x.shape[0]
    tm = TOKEN_TILE
    row = lambda i: (i, 0)
    fixed = lambda i: (0, 0)
    gw = 3 * A_WIDTH
    return pl.pallas_call(
        _proj_kernel,
        grid=(t // tm,),
        in_specs=[pl.BlockSpec((tm, D_MODEL), row), pl.BlockSpec((1, D_MODEL), fixed),
                  pl.BlockSpec((D_MODEL, N_IN), fixed)],
        out_specs=[_residue_spec(seq, dil, gw) for _, dil in A_GROUPS]
                  + [pl.BlockSpec((tm, B_Q_WIDTH), row), pl.BlockSpec((tm, 2 * B_KV_WIDTH), row)],
        out_shape=[jax.ShapeDtypeStruct((bsz, dil, seq // dil, gw), BF16) for _, dil in A_GROUPS]
                  + [jax.ShapeDtypeStruct((t, B_Q_WIDTH), BF16), jax.ShapeDtypeStruct((t, 2 * B_KV_WIDTH), BF16)],
        scratch_shapes=[pltpu.VMEM((gw // LANES, tm, LANES), F32),
                        pltpu.VMEM((gw // LANES, ROW_STRIDE, tm // ROW_STRIDE, LANES), F32)],
        compiler_params=_params("parallel"),
        name="proj",
    )(x, g, w_in)


BAND_CASES = 3


def _band(q0, n, kw, half_window):
    start = pl.multiple_of(jnp.clip(q0 - half_window, 0, n - kw), half_window)
    return start, (q0 - start) // half_window


def _fill_bias(bias_ref, slopes, tq, kw, half_window):
    rel = lax.broadcasted_iota(jnp.int32, (tq, kw), 0) - lax.broadcasted_iota(jnp.int32, (tq, kw), 1)
    for case in range(BAND_CASES):
        dist = jnp.abs(rel + case * half_window)
        for h, slope in enumerate(slopes):
            bias_ref[case, h] = jnp.where(dist <= half_window, -slope * dist.astype(F32), NEG_INF)


Q_SCALE = HEAD_DIM ** -0.5


def _head_scores(qh, kh, bias):
    return lax.dot_general(qh, kh, (((1,), (1,)), ((), ())), preferred_element_type=F32) + bias


def _attn_a_shapes(n, dil):
    tq = min(ATTN_Q_TILE, n)
    res = max(1, min(dil, ATTN_A_ROWS // n))
    return tq, min(n, tq + 2 * A_HALF_WINDOW), res, min(ATTN_A_BLOCKS, res * (n // tq))


def _attn_a_kernel(x_ref, o_ref, lse_ref, s_ref, p_ref, bias_ref, *, n, dil, slopes):
    tq, kw, res, nb = _attn_a_shapes(n, dil)
    per_res = n // tq
    lane = lax.broadcasted_iota(jnp.int32, (tq, LANES), 1)
    lane_head = lax.broadcasted_iota(jnp.int32, (tq, A_WIDTH), 1) // HEAD_DIM
    n_pairs = nb * A_HEADS

    @pl.when((pl.program_id(0) == 0) & (pl.program_id(1) == 0))
    def _():
        _fill_bias(bias_ref, slopes, tq, kw, A_HALF_WINDOW)

    def step(it, carry):
        blocks = []
        for j in range(nb):
            blk = it * nb + j
            r = blk // per_res
            q0 = pl.multiple_of((blk % per_res) * tq, tq)
            start, case = _band(q0, n, kw, A_HALF_WINDOW)
            blocks.append((r, q0, case, x_ref[0, r, pl.ds(q0, tq), 0:A_WIDTH] * Q_SCALE,
                           x_ref[0, r, pl.ds(start, kw), A_WIDTH:2 * A_WIDTH],
                           x_ref[0, r, pl.ds(start, kw), 2 * A_WIDTH:3 * A_WIDTH]))
        for j, (_, _, case, q, k, _) in enumerate(blocks):
            q_heads = jnp.concatenate([jnp.where(lane_head == h, q, jnp.zeros_like(q)) for h in range(A_HEADS)],
                                      axis=0)
            s_all = lax.dot_general(q_heads, k, (((1,), (1,)), ((), ())), preferred_element_type=F32)
            for h in range(A_HEADS):
                s_ref[j * A_HEADS + h] = s_all[h * tq:(h + 1) * tq] + bias_ref[case, h]
        ms = [jnp.max(s_ref[i], axis=-1, keepdims=True) for i in range(n_pairs)]
        ls = []
        for i in range(n_pairs):
            p = jnp.exp(s_ref[i] - ms[i])
            ls.append(jnp.sum(p, axis=-1, keepdims=True))
            p_ref[i] = p.astype(BF16)
        for j in range(nb):
            r, q0, v = blocks[j][0], blocks[j][1], blocks[j][5]
            pv = _dot(p_ref[pl.ds(j * A_HEADS, A_HEADS)].reshape(A_HEADS * tq, kw), v)
            out = jnp.zeros((tq, A_WIDTH), F32)
            lse_tile = jnp.zeros((tq, LANES), F32)
            for h in range(A_HEADS):
                i = j * A_HEADS + h
                out = jnp.where(lane_head == h, pv[h * tq:(h + 1) * tq] / ls[i], out)
                lse_tile = jnp.where(lane == h, ms[i] + jnp.log(ls[i]), lse_tile)
            o_ref[0, r, pl.ds(q0, tq), :] = out.astype(BF16)
            lse_ref[0, r, pl.ds(q0, tq), :] = lse_tile
        return carry

    lax.fori_loop(0, res * per_res // nb, step, 0)


def _attn_a(pa, gi):
    window, dil = A_GROUPS[gi]
    assert window // (2 * dil) == A_HALF_WINDOW
    bsz, _, n, gw = pa.shape
    tq, kw, res, nb = _attn_a_shapes(n, dil)
    assert dil % res == 0 and (res * (n // tq)) % nb == 0
    slopes = _alibi_slopes(len(A_GROUPS) * A_HEADS).reshape(len(A_GROUPS), A_HEADS)[gi]
    slopes = tuple(float(s * np.float32(dil)) for s in slopes)
    block = lambda w: pl.BlockSpec((1, res, n, w), lambda b, r: (b, r, 0, 0))
    return pl.pallas_call(
        functools.partial(_attn_a_kernel, n=n, dil=dil, slopes=slopes),
        grid=(bsz, dil // res),
        in_specs=[block(gw)],
        out_specs=[block(A_WIDTH), block(LANES)],
        out_shape=[jax.ShapeDtypeStruct((bsz, dil, n, A_WIDTH), BF16),
                   jax.ShapeDtypeStruct((bsz, dil, n, LANES), F32)],
        scratch_shapes=[pltpu.VMEM((nb * A_HEADS, tq, kw), F32), pltpu.VMEM((nb * A_HEADS, tq, kw), BF16),
                        pltpu.VMEM((BAND_CASES, A_HEADS, tq, kw), F32)],
        compiler_params=_params("arbitrary", "arbitrary"),
        name=f"attn_a{gi}",
    )(pa)


def _attn_b_kernel(sink_ref, q_ref, kv_ref, o_ref, s_ref, p_ref, bias_ref, *, n, slopes):
    hw = B_HALF_WINDOW
    tq = min(ATTN_Q_TILE, n)
    kw = min(n, tq + 2 * hw)
    heads = range(B_Q_HEADS)
    low = lax.broadcasted_iota(jnp.int32, (tq, LANES), 1) < HEAD_DIM
    swap = lambda t: jnp.concatenate([t[:, HEAD_DIM:], t[:, :HEAD_DIM]], axis=1)

    @pl.when(pl.program_id(0) == 0)
    def _():
        _fill_bias(bias_ref, slopes, tq, kw, hw)

    nb = min(ATTN_B_BLOCKS, n // tq)
    pairs = [(j, hq) for j in range(nb) for hq in heads]

    def step(it, carry):
        blocks = []
        for j in range(nb):
            q0 = pl.multiple_of((it * nb + j) * tq, tq)
            start, case = _band(q0, n, kw, hw)
            blocks.append((q0, case, q_ref[0, pl.ds(q0, tq), :] * Q_SCALE,
                           kv_ref[0, pl.ds(start, kw), 0:B_KV_WIDTH],
                           kv_ref[0, pl.ds(start, kw), B_KV_WIDTH:2 * B_KV_WIDTH]))
        for j, (_, case, q, k2, _) in enumerate(blocks):
            for hk in range(B_KV_HEADS):
                rows = []
                for hq in range(hk * B_GROUP, (hk + 1) * B_GROUP):
                    tile = q[:, (hq // 2) * LANES:(hq // 2 + 1) * LANES]
                    moved = tile if hq % 2 == hk else swap(tile)
                    rows.append(jnp.where(low == (hk == 0), moved, jnp.zeros_like(moved)))
                s_all = lax.dot_general(jnp.concatenate(rows, axis=0), k2, (((1,), (1,)), ((), ())),
                                        preferred_element_type=F32)
                for g in range(B_GROUP):
                    hq = hk * B_GROUP + g
                    s_ref[j * B_Q_HEADS + hq] = s_all[g * tq:(g + 1) * tq] + bias_ref[case, hq]
        ms = [jnp.maximum(jnp.max(s_ref[i], axis=-1, keepdims=True), sink_ref[hq]) for i, (_, hq) in enumerate(pairs)]
        ls = []
        for i, (_, hq) in enumerate(pairs):
            p = jnp.exp(s_ref[i] - ms[i])
            ls.append(jnp.sum(p, axis=-1, keepdims=True) + jnp.exp(sink_ref[hq] - ms[i]))
            p_ref[i] = p.astype(BF16)
        for j, (q0, _, _, _, v2) in enumerate(blocks):
            placed = []
            for hk in range(B_KV_HEADS):
                first = j * B_Q_HEADS + hk * B_GROUP
                pv = _dot(p_ref[pl.ds(first, B_GROUP)].reshape(B_GROUP * tq, kw), v2)
                for g in range(B_GROUP):
                    hq = hk * B_GROUP + g
                    o = pv[g * tq:(g + 1) * tq] / ls[first + g]
                    placed.append(o if hq % 2 == hk else swap(o))
            tiles = [jnp.where(low, placed[t], placed[t + 1]) for t in range(0, B_Q_HEADS, 2)]
            o_ref[0, pl.ds(q0, tq), :] = jnp.concatenate(tiles, axis=1).astype(BF16)
        return carry

    lax.fori_loop(0, n // (tq * nb), step, 0)


def _attn_b(qb, kvb, sink, bsz, seq):
    slopes = tuple(float(s) for s in _alibi_slopes(B_Q_HEADS))
    tq = min(ATTN_Q_TILE, seq)
    kw = min(seq, tq + 2 * B_HALF_WINDOW)
    nb = min(ATTN_B_BLOCKS, seq // tq)
    assert seq % (tq * nb) == 0
    out = pl.pallas_call(
        functools.partial(_attn_b_kernel, n=seq, slopes=slopes),
        grid=(bsz,),
        in_specs=[pl.BlockSpec(memory_space=pltpu.SMEM),
                  pl.BlockSpec((1, seq, B_Q_WIDTH), lambda b: (b, 0, 0)),
                  pl.BlockSpec((1, seq, 2 * B_KV_WIDTH), lambda b: (b, 0, 0))],
        out_specs=pl.BlockSpec((1, seq, B_Q_WIDTH), lambda b: (b, 0, 0)),
        out_shape=jax.ShapeDtypeStruct((bsz, seq, B_Q_WIDTH), BF16),
        scratch_shapes=[pltpu.VMEM((nb * B_Q_HEADS, tq, kw), F32), pltpu.VMEM((nb * B_Q_HEADS, tq, kw), BF16),
                        pltpu.VMEM((BAND_CASES, B_Q_HEADS, tq, kw), F32)],
        compiler_params=_params("arbitrary"),
        name="attn_b",
    )(sink, qb.reshape(bsz, seq, B_Q_WIDTH), kvb.reshape(bsz, seq, 2 * B_KV_WIDTH))
    return out.reshape(bsz * seq, B_Q_WIDTH)


def _split_bf16(a):
    hi = a.astype(BF16)
    return hi, (a - hi.astype(F32)).astype(BF16)


def _top2(h2, wr_a, wr_b):
    tm = h2.shape[0]
    nt = (((1,), (1,)), ((), ()))
    h_hi, h_lo = _split_bf16(h2)
    by_hi = lax.dot_general(wr_a, h_hi, nt, preferred_element_type=F32)
    by_lo = lax.dot_general(wr_b, h_lo, nt, preferred_element_type=F32)
    logits = by_hi[:N_EXPERTS] + (by_hi[N_EXPERTS:] + by_lo[:N_EXPERTS])
    row = lax.broadcasted_iota(jnp.int32, logits.shape, 0)
    m1 = jnp.max(logits, axis=0, keepdims=True)
    i1 = jnp.min(jnp.where(logits == m1, row, N_EXPERTS), axis=0, keepdims=True)
    l2 = jnp.where(row == i1, -jnp.inf, logits)
    m2 = jnp.max(l2, axis=0, keepdims=True)
    i2 = jnp.min(jnp.where(l2 == m2, row, N_EXPERTS), axis=0, keepdims=True)
    e2 = jnp.exp(m2 - m1)
    den = 1.0 + e2
    packed = jnp.where(row == 0, i1.astype(F32),
                       jnp.where(row == 1, i2.astype(F32),
                                 jnp.where(row == 2, 1.0 / den, jnp.where(row == 3, e2 / den, 0.0))))
    padded = jnp.concatenate([packed, jnp.zeros((LANES - N_EXPERTS, tm), F32)], axis=0)
    return padded.T


def _interleave_rows(dst_ref, tmp_ref, piece, dil):
    rows = TOKEN_TILE // dil
    if dil <= ROW_STRIDE:
        for r in range(dil):
            dst_ref[pl.ds(r, rows, stride=dil), :] = piece(r)
        return
    outer = dil // ROW_STRIDE
    for r in range(dil):
        tmp_ref[r % ROW_STRIDE, pl.ds(r // ROW_STRIDE, rows, stride=outer), :] = piece(r)
    for r1 in range(ROW_STRIDE):
        dst_ref[pl.ds(r1, TOKEN_TILE // ROW_STRIDE, stride=ROW_STRIDE), :] = tmp_ref[r1]


def _merge_kernel(*refs, moe):
    ng = len(A_GROUPS)
    x_ref = refs[0]
    o_refs, l_refs = refs[1:1 + ng], refs[1 + ng:1 + 2 * ng]
    yb_ref, gmix_ref, wgate_ref, bgate_ref, wpa_ref, wpb_ref, wout_ref, gffn_ref = refs[1 + 2 * ng:9 + 2 * ng]
    rest = refs[9 + 2 * ng:]
    if moe:
        wr_hi_ref, wr_lo_ref, xo_ref, h2_ref, rout_ref, so_ref, sl_ref, tmp_ref = rest
    else:
        xo_ref, h2_ref, so_ref, sl_ref, tmp_ref = rest
    x = x_ref[...]
    h = _rms(x, gmix_ref[...]).astype(BF16)
    z_b = _dot(yb_ref[...], wpb_ref[...])
    g_a = jax.nn.sigmoid(_dot(h, wgate_ref[:, :D_MODEL]) + bgate_ref[:, :D_MODEL])
    g_b = jax.nn.sigmoid(_dot(h, wgate_ref[:, D_MODEL:]) + bgate_ref[:, D_MODEL:])
    outs, lses = [], []
    for gi, (_, dil) in enumerate(A_GROUPS):
        if dil == 1:
            outs.append(o_refs[gi][0, 0].astype(F32))
            lses.append(l_refs[gi][0, 0])
        else:
            chunks = A_WIDTH // LANES
            for c in range(chunks):
                piece = lambda r, c=c, gi=gi: o_refs[gi][0, r, :, c * LANES:(c + 1) * LANES].astype(F32)
                _interleave_rows(so_ref.at[gi, c], tmp_ref.at[c], piece, dil)
            _interleave_rows(sl_ref.at[gi], tmp_ref.at[chunks], lambda r, gi=gi: l_refs[gi][0, r], dil)
            outs.append(jnp.concatenate([so_ref[gi, c] for c in range(chunks)], axis=-1))
            lses.append(sl_ref[gi])
    mx = jnp.maximum(jnp.maximum(lses[0], lses[1]), lses[2])
    es = [jnp.exp(l - mx) for l in lses]
    den = es[0] + es[1] + es[2]
    heads = []
    for hh in range(A_HEADS):
        cols = slice(hh * HEAD_DIM, (hh + 1) * HEAD_DIM)
        acc = None
        for g in range(ng):
            term = (es[g][:, hh:hh + 1] / den[:, hh:hh + 1]) * outs[g][:, cols]
            acc = term if acc is None else acc + term
        heads.append(acc)
    y_a = jnp.concatenate(heads, axis=-1).astype(BF16)
    merged = g_a * _dot(y_a, wpa_ref[...]) + g_b * z_b
    xn = x + _dot(merged.astype(BF16), wout_ref[...])
    xo_ref[...] = xn
    h2 = _rms(xn, gffn_ref[...])
    h2_ref[...] = h2.astype(h2_ref.dtype)
    if moe:
        rout_ref[...] = _top2(h2, wr_hi_ref[...], wr_lo_ref[...])


def _merge(x, outs, lses, y_b, seq, gmix, w_gate, b_gate, w_pa, w_pb, w_out, gffn, w_router=None):
    t = x.shape[0]
    tm = TOKEN_TILE
    moe = w_router is not None
    row = lambda i: (i, 0)
    fixed = lambda i: (0, 0)
    full = lambda a: pl.BlockSpec(a.shape, fixed)
    weights = [gmix, w_gate, b_gate, w_pa, w_pb, w_out, gffn]
    out_specs = [pl.BlockSpec((tm, D_MODEL), row), pl.BlockSpec((tm, D_MODEL), row)]
    out_shape = [jax.ShapeDtypeStruct((t, D_MODEL), F32), jax.ShapeDtypeStruct((t, D_MODEL), F32 if moe else BF16)]
    if moe:
        wr_hi, wr_lo = _split_bf16(w_router.T)
        weights.extend([jnp.concatenate([wr_hi, wr_lo]), jnp.concatenate([wr_hi, jnp.zeros_like(wr_hi)])])
        out_specs.append(pl.BlockSpec((tm, LANES), row))
        out_shape.append(jax.ShapeDtypeStruct((t, LANES), F32))
    ng = len(A_GROUPS)
    return pl.pallas_call(
        functools.partial(_merge_kernel, moe=moe),
        grid=(t // tm,),
        in_specs=[pl.BlockSpec((tm, D_MODEL), row)]
                 + [_residue_spec(seq, dil, A_WIDTH) for _, dil in A_GROUPS]
                 + [_residue_spec(seq, dil, LANES) for _, dil in A_GROUPS]
                 + [pl.BlockSpec((tm, B_Q_WIDTH), row)] + [full(a) for a in weights],
        out_specs=out_specs,
        out_shape=out_shape,
        scratch_shapes=[pltpu.VMEM((ng, A_WIDTH // LANES, tm, LANES), F32), pltpu.VMEM((ng, tm, LANES), F32),
                        pltpu.VMEM((A_WIDTH // LANES + 1, ROW_STRIDE, tm // ROW_STRIDE, LANES), F32)],
        compiler_params=_params("parallel"),
        name="merge",
    )(x, *outs, *lses, y_b, *weights)


def _swiglu_step(xb, wg, wu, wd):
    a = _dot(xb, wg)
    return _dot((jax.nn.silu(a) * _dot(xb, wu)).astype(BF16), wd)


def _ffn_kernel(h_ref, x_ref, wg_ref, wu_ref, wd_ref, o_ref, hm_ref):
    h = h_ref[...]
    d_ff = wg_ref.shape[1]
    for lo in range(0, d_ff, FF_CHUNK_DENSE):
        cols = slice(lo, min(lo + FF_CHUNK_DENSE, d_ff))
        hm_ref[:, cols] = (jax.nn.silu(_dot(h, wg_ref[:, cols])) * _dot(h, wu_ref[:, cols])).astype(BF16)
    o_ref[...] = x_ref[...] + _dot(hm_ref[...], wd_ref[...])


def _ffn(h2, x, w_g, w_u, w_d):
    t = x.shape[0]
    tm = TOKEN_TILE
    d_ff = w_g.shape[1]
    row = lambda i: (i, 0)
    fixed = lambda i: (0, 0)
    return pl.pallas_call(
        _ffn_kernel,
        grid=(t // tm,),
        in_specs=[pl.BlockSpec((tm, D_MODEL), row), pl.BlockSpec((tm, D_MODEL), row),
                  pl.BlockSpec((D_MODEL, d_ff), fixed), pl.BlockSpec((D_MODEL, d_ff), fixed),
                  pl.BlockSpec((d_ff, D_MODEL), fixed)],
        out_specs=pl.BlockSpec((tm, D_MODEL), row),
        out_shape=jax.ShapeDtypeStruct((t, D_MODEL), F32),
        scratch_shapes=[pltpu.VMEM((tm, d_ff), BF16)],
        compiler_params=_params("parallel"),
        name="ffn_dense",
    )(h2, x, w_g, w_u, w_d)


def _dispatch_kernel(zero_ref, dest_ref, h_hbm, xs_ref, hbuf_ref, zbuf_ref, in_sem, out_sem, zsem):
    tm = hbuf_ref.shape[1]
    i = pl.program_id(0)
    steps = pl.num_programs(0)
    load = lambda step, slot: pltpu.make_async_copy(
        h_hbm.at[pl.ds(pl.multiple_of(step * tm, tm), tm), :], hbuf_ref.at[slot], in_sem.at[slot])

    def wait_rows(slot):
        for _ in range(TOP_K):
            pltpu.make_async_copy(hbuf_ref.at[slot], xs_ref.at[pl.ds(0, tm), :], out_sem.at[slot]).wait()

    @pl.when(i == 0)
    def _():
        load(0, 0).start()
        zbuf_ref[...] = jnp.zeros_like(zbuf_ref)
        n_zero = zero_ref.shape[0]
        zero_copy = lambda z: pltpu.make_async_copy(
            zbuf_ref, xs_ref.at[pl.ds(pl.multiple_of(zero_ref[z], MOE_TILE), MOE_TILE), :], zsem)
        for z in range(n_zero):
            zero_copy(z).start()
            zero_copy(z).wait()

    for slot in range(2):
        @pl.when(i % 2 == slot)
        def _():
            load(i, slot).wait()

            @pl.when(i >= 1)
            def _():
                wait_rows(1 - slot)

            @pl.when(i + 1 < steps)
            def _():
                load(i + 1, 1 - slot).start()

            for r in range(tm):
                for k in range(TOP_K):
                    d = dest_ref[0, 0, TOP_K * r + k]
                    pltpu.make_async_copy(hbuf_ref.at[slot, pl.ds(r, 1), :], xs_ref.at[pl.ds(d, 1), :],
                                          out_sem.at[slot]).start(priority=k)

            @pl.when(i == steps - 1)
            def _():
                wait_rows(slot)


def _dispatch(h2, dest, zero_blocks, cap):
    t = h2.shape[0]
    tm = DISPATCH_TILE
    return pl.pallas_call(
        _dispatch_kernel,
        grid_spec=pltpu.PrefetchScalarGridSpec(
            num_scalar_prefetch=1,
            grid=(t // tm,),
            in_specs=[pl.BlockSpec((1, 1, TOP_K * tm), lambda i, z: (i, 0, 0), memory_space=pltpu.SMEM),
                      pl.BlockSpec(memory_space=pl.ANY)],
            out_specs=pl.BlockSpec(memory_space=pl.ANY),
            scratch_shapes=[pltpu.VMEM((2, tm, D_MODEL), F32), pltpu.VMEM((MOE_TILE, D_MODEL), F32),
                            pltpu.SemaphoreType.DMA((2,)), pltpu.SemaphoreType.DMA((2,)),
                            pltpu.SemaphoreType.DMA(())],
        ),
        out_shape=jax.ShapeDtypeStruct((cap, D_MODEL), F32),
        compiler_params=_params("arbitrary"),
        name="dispatch",
    )(zero_blocks, dest.reshape(t // tm, 1, TOP_K * tm), h2)


def _expert_kernel(be_ref, used_ref, xs_ref, wg_ref, wu_ref, wd_ref, ys_ref, xb_ref, acc_ref):
    del be_ref
    i, j = pl.program_id(0), pl.program_id(1)
    last = pl.num_programs(1) - 1
    active = i < used_ref[0]

    @pl.when(active & (j == 0))
    def _():
        xb_ref[...] = xs_ref[...].astype(BF16)

    @pl.when(active)
    def _():
        part = _swiglu_step(xb_ref[...], wg_ref[0], wu_ref[0], wd_ref[0])

        @pl.when(j == 0)
        def _():
            acc_ref[...] = part

        @pl.when(j > 0)
        def _():
            acc_ref[...] += part

        @pl.when(j == last)
        def _():
            ys_ref[...] = acc_ref[...]

    @pl.when(jnp.logical_not(active) & (j == last))
    def _():
        ys_ref[...] = jnp.zeros_like(ys_ref)


def _experts(xs, block_e, n_used, w_g, w_u, w_d):
    cap = xs.shape[0]
    tm = MOE_TILE
    d_ff = w_g.shape[2]
    tf = d_ff // FF_SPLIT_EXPERT
    assert tf * FF_SPLIT_EXPERT == d_ff and tf % LANES == 0
    chunk = lambda i, j, used: jnp.where(i < used[0], j, FF_SPLIT_EXPERT - 1)
    return pl.pallas_call(
        _expert_kernel,
        grid_spec=pltpu.PrefetchScalarGridSpec(
            num_scalar_prefetch=2,
            grid=(cap // tm, FF_SPLIT_EXPERT),
            in_specs=[pl.BlockSpec((tm, D_MODEL), lambda i, j, be, used: (i, 0)),
                      pl.BlockSpec((1, D_MODEL, tf), lambda i, j, be, used: (be[i], 0, chunk(i, j, used))),
                      pl.BlockSpec((1, D_MODEL, tf), lambda i, j, be, used: (be[i], 0, chunk(i, j, used))),
                      pl.BlockSpec((1, tf, D_MODEL), lambda i, j, be, used: (be[i], chunk(i, j, used), 0))],
            out_specs=pl.BlockSpec((tm, D_MODEL), lambda i, j, be, used: (i, 0)),
            scratch_shapes=[pltpu.VMEM((tm, D_MODEL), BF16), pltpu.VMEM((tm, D_MODEL), F32)],
        ),
        out_shape=jax.ShapeDtypeStruct((cap, D_MODEL), F32),
        compiler_params=_params("arbitrary", "arbitrary"),
        name="experts",
    )(block_e, n_used, xs, w_g, w_u, w_d)


def _combine_kernel(dest_ref, dest_next_ref, r_ref, x_ref, ys_ref, g_ref, o_ref, y_ref, sem):
    tm = x_ref.shape[0]
    i = pl.program_id(0)

    def fetch(idx_ref, slot):
        for r in range(tm):
            for k in range(TOP_K):
                d = idx_ref[0, 0, TOP_K * r + k]
                pltpu.make_async_copy(ys_ref.at[pl.ds(d, 1), :], y_ref.at[slot, k, pl.ds(r, 1), :],
                                      sem.at[slot, k]).start(priority=k)

    @pl.when(i == 0)
    def _():
        fetch(dest_ref, 0)

    for parity in range(2):
        @pl.when((i + 1 < pl.num_programs(0)) & ((i + 1) % 2 == parity))
        def _():
            fetch(dest_next_ref, parity)

    slot = i % 2
    for k in range(TOP_K):
        pltpu.make_async_copy(ys_ref.at[pl.ds(0, tm), :], y_ref.at[slot, k], sem.at[slot, k]).wait()
    gates = r_ref[...]
    moe = gates[:, 2:3] * y_ref[slot, 0] + gates[:, 3:4] * y_ref[slot, 1]
    o_ref[...] = _rms(x_ref[...] + moe, g_ref[...])


def _combine(ys, dest, rout, x, g_final):
    t = x.shape[0]
    tm = COMBINE_TILE
    steps = t // tm
    dest3 = dest.reshape(steps, 1, TOP_K * tm)
    return pl.pallas_call(
        _combine_kernel,
        grid=(steps,),
        in_specs=[pl.BlockSpec((1, 1, TOP_K * tm), lambda i: (i, 0, 0), memory_space=pltpu.SMEM),
                  pl.BlockSpec((1, 1, TOP_K * tm), lambda i: (jnp.minimum(i + 1, steps - 1), 0, 0),
                               memory_space=pltpu.SMEM),
                  pl.BlockSpec((tm, LANES), lambda i: (i, 0)),
                  pl.BlockSpec((tm, D_MODEL), lambda i: (i, 0)),
                  pl.BlockSpec(memory_space=pl.ANY),
                  pl.BlockSpec((1, D_MODEL), lambda i: (0, 0))],
        out_specs=pl.BlockSpec((tm, D_MODEL), lambda i: (i, 0)),
        out_shape=jax.ShapeDtypeStruct((t, D_MODEL), F32),
        scratch_shapes=[pltpu.VMEM((2, TOP_K, tm, D_MODEL), F32), pltpu.SemaphoreType.DMA((2, TOP_K))],
        compiler_params=_params("arbitrary"),
        name="combine",
    )(dest3, dest3, rout, x, ys, g_final)


def _routing_tables(rout, t):
    tm = MOE_TILE
    e_flat = rout[:, :TOP_K].astype(jnp.int32).reshape(-1)
    onehot = (e_flat[:, None] == jnp.arange(N_EXPERTS, dtype=jnp.int32)[None, :]).astype(jnp.int32)
    csum = jnp.cumsum(onehot, axis=0)
    counts = csum[-1]
    padded = (counts + tm - 1) // tm * tm
    pend = jnp.cumsum(padded)
    pstart = pend - padded
    dest = jnp.sum(onehot * (csum - 1 + pstart[None, :]), axis=1).astype(jnp.int32)
    n_blocks = -(-(t * TOP_K) // tm) + N_EXPERTS
    block_start = jnp.arange(n_blocks, dtype=jnp.int32) * tm
    block_e = jnp.minimum(jnp.sum((pend[None, :] <= block_start[:, None]).astype(jnp.int32), axis=1), N_EXPERTS - 1)
    zero_blocks = jnp.concatenate([jnp.maximum(pend - tm, 0), block_start[n_blocks - N_EXPERTS:]]).astype(jnp.int32)
    n_used = (pend[N_EXPERTS - 1:] // tm).astype(jnp.int32)
    return dest, block_e, n_used, zero_blocks, n_blocks * tm


def _trunk(x3, wts):
    bsz, seq, _ = x3.shape
    assert seq % TOKEN_TILE == 0
    x = x3.reshape(bsz * seq, D_MODEL)
    depth = len(wts["layers"])
    for li, lw in enumerate(wts["layers"]):
        *pas, qb, kvb = _proj(x, lw["norm_mix_g"], lw["w_in"], bsz, seq)
        outs, lses = zip(*[_attn_a(pa, gi) for gi, pa in enumerate(pas)])
        y_b = _attn_b(qb, kvb, lw["sink"], bsz, seq)
        mixer = (lw["norm_mix_g"], lw["w_gate"], lw["b_gate"], lw["w_proj_a"], lw["w_proj_b"], lw["w_out"],
                 lw["norm_ffn_g"])
        if "w_router" not in lw:
            assert li < depth - 1
            x, h2 = _merge(x, outs, lses, y_b, seq, *mixer)
            x = _ffn(h2, x, lw["w_ff_gate"], lw["w_ff_up"], lw["w_ff_down"])
        else:
            assert li == depth - 1
            x, h2, rout = _merge(x, outs, lses, y_b, seq, *mixer, w_router=lw["w_router"])
            dest, block_e, n_used, zero_blocks, cap = _routing_tables(rout, bsz * seq)
            xs = _dispatch(h2, dest, zero_blocks, cap)
            ys = _experts(xs, block_e, n_used, lw["w_e_gate"], lw["w_e_up"], lw["w_e_down"])
            x = _combine(ys, dest, rout, x, wts["norm_final_g"])
    return x.reshape(bsz, seq, D_MODEL)


def kernel(x_prompt, x_sample, norm_mix_g, w_in, w_gate, b_gate, w_proj_a, w_proj_b, w_out, sink, norm_ffn_g,
           w_ff_gate, w_ff_up, w_ff_down, w_router, w_e_gate, w_e_up, w_e_down, norm_final_g):
    depth = w_in.shape[0]
    layers = []
    for li in range(depth):
        lw = {
            "norm_mix_g": norm_mix_g[li].reshape(1, D_MODEL), "w_in": w_in[li].astype(BF16),
            "w_gate": w_gate[li].astype(BF16), "b_gate": b_gate[li].reshape(1, 2 * D_MODEL),
            "w_proj_a": w_proj_a[li].astype(BF16), "w_proj_b": w_proj_b[li].astype(BF16),
            "w_out": w_out[li].astype(BF16), "sink": sink[li], "norm_ffn_g": norm_ffn_g[li].reshape(1, D_MODEL),
        }
        i = li // 2
        if li % 2 == 0:
            lw.update(w_ff_gate=w_ff_gate[i].astype(BF16), w_ff_up=w_ff_up[i].astype(BF16),
                      w_ff_down=w_ff_down[i].astype(BF16))
        else:
            lw.update(w_router=w_router[i], w_e_gate=w_e_gate[i].astype(BF16), w_e_up=w_e_up[i].astype(BF16),
                      w_e_down=w_e_down[i].astype(BF16))
        layers.append(lw)
    wts = {"layers": layers, "norm_final_g": norm_final_g.reshape(1, D_MODEL)}
    return _trunk(x_prompt, wts), _trunk(x_sample, wts)
```

```python
import functools

import numpy as np
import jax
import jax.numpy as jnp
from jax import lax
from jax.experimental import pallas as pl
from jax.experimental.pallas import tpu as pltpu

F32 = jnp.float32
BF16 = jnp.bfloat16

D_MODEL = 1024
HEAD_DIM = 64
A_GROUPS = ((128, 1), (512, 4), (2048, 16))
A_HALF_WINDOW = 64
A_HEADS = 4
A_WIDTH = A_HEADS * HEAD_DIM
A_COLS = len(A_GROUPS) * 3 * A_WIDTH
B_Q_HEADS = 8
B_KV_HEADS = 2
B_GROUP = B_Q_HEADS // B_KV_HEADS
B_Q_WIDTH = B_Q_HEADS * HEAD_DIM
B_KV_WIDTH = B_KV_HEADS * HEAD_DIM
B_HALF_WINDOW = 128
N_IN = A_COLS + B_Q_WIDTH + 2 * B_KV_WIDTH
N_EXPERTS = 8
TOP_K = 2
RMS_EPS = 1e-6
NEG_INF = -1e30

LANES = 128
TOKEN_TILE = 512
ATTN_Q_TILE = 128
ATTN_A_ROWS = 2048
ATTN_A_BLOCKS = 4
ATTN_B_BLOCKS = 2
FF_CHUNK_DENSE = 1536
FF_SPLIT_EXPERT = 2
MOE_TILE = 512
DISPATCH_TILE = 512
COMBINE_TILE = 512
ROW_STRIDE = 4
VMEM_LIMIT = 52 * 1024 * 1024


def _alibi_slopes(n):
    return np.asarray(2.0 ** (-8.0 * np.arange(1, n + 1) / n), dtype=np.float32)


def _params(*sem):
    return pltpu.CompilerParams(dimension_semantics=sem, vmem_limit_bytes=VMEM_LIMIT)


def _rms(x, g):
    return x * lax.rsqrt(jnp.mean(x * x, axis=-1, keepdims=True) + RMS_EPS) * g


def _dot(a, b):
    return jnp.dot(a, b, preferred_element_type=F32)


def _residue_spec(seq, dil, width):
    tiles_per_seq = seq // TOKEN_TILE
    return pl.BlockSpec((1, dil, TOKEN_TILE // dil, width),
                        lambda i: (i // tiles_per_seq, 0, i % tiles_per_seq, 0))


def _proj_kernel(x_ref, g_ref, w_ref, *refs):
    pa_refs, (qb_ref, kv_ref, scr_ref, tmp_ref) = refs[:len(A_GROUPS)], refs[len(A_GROUPS):]
    h = _rms(x_ref[...], g_ref[...]).astype(BF16)
    for gi, (_, dil) in enumerate(A_GROUPS):
        cols = slice(gi * 3 * A_WIDTH, (gi + 1) * 3 * A_WIDTH)
        res = _dot(h, w_ref[:, cols])
        if dil == 1:
            pa_refs[gi][0, 0] = res.astype(BF16)
        else:
            chunks = res.shape[1] // LANES
            for c in range(chunks):
                scr_ref[c] = res[:, c * LANES:(c + 1) * LANES]
            rows = TOKEN_TILE // dil
            if dil <= ROW_STRIDE:
                src = lambda c, r: scr_ref[c, pl.ds(r, rows, stride=dil), :]
            else:
                outer = dil // ROW_STRIDE
                for c in range(chunks):
                    for r1 in range(ROW_STRIDE):
                        tmp_ref[c, r1] = scr_ref[c, pl.ds(r1, TOKEN_TILE // ROW_STRIDE, stride=ROW_STRIDE), :]
                src = lambda c, r: tmp_ref[c, r % ROW_STRIDE, pl.ds(r // ROW_STRIDE, rows, stride=outer), :]
            for r in range(dil):
                parts = [src(c, r) for c in range(chunks)]
                pa_refs[gi][0, r] = jnp.concatenate(parts, axis=-1).astype(BF16)
    qb_ref[...] = _dot(h, w_ref[:, A_COLS:A_COLS + B_Q_WIDTH]).astype(BF16)
    kv_ref[...] = _dot(h, w_ref[:, A_COLS + B_Q_WIDTH:]).astype(BF16)


def _proj(x, g, w_in, bsz, seq):
    t = x.shape[0]
    tm = TOKEN_TILE
    row = lambda i: (i, 0)
    fixed = lambda i: (0, 0)
    gw = 3 * A_WIDTH
    return pl.pallas_call(
        _proj_kernel,
        grid=(t // tm,),
        in_specs=[pl.BlockSpec((tm, D_MODEL), row), pl.BlockSpec((1, D_MODEL), fixed),
                  pl.BlockSpec((D_MODEL, N_IN), fixed)],
        out_specs=[_residue_spec(seq, dil, gw) for _, dil in A_GROUPS]
                  + [pl.BlockSpec((tm, B_Q_WIDTH), row), pl.BlockSpec((tm, 2 * B_KV_WIDTH), row)],
        out_shape=[jax.ShapeDtypeStruct((bsz, dil, seq // dil, gw), BF16) for _, dil in A_GROUPS]
                  + [jax.ShapeDtypeStruct((t, B_Q_WIDTH), BF16), jax.ShapeDtypeStruct((t, 2 * B_KV_WIDTH), BF16)],
        scratch_shapes=[pltpu.VMEM((gw // LANES, tm, LANES), F32),
                        pltpu.VMEM((gw // LANES, ROW_STRIDE, tm // ROW_STRIDE, LANES), F32)],
        compiler_params=_params("parallel"),
        name="proj",
    )(x, g, w_in)


BAND_CASES = 3


def _band(q0, n, kw, half_window):
    start = pl.multiple_of(jnp.clip(q0 - half_window, 0, n - kw), half_window)
    return start, (q0 - start) // half_window


def _fill_bias(bias_ref, slopes, tq, kw, half_window):
    rel = lax.broadcasted_iota(jnp.int32, (tq, kw), 0) - lax.broadcasted_iota(jnp.int32, (tq, kw), 1)
    for case in range(BAND_CASES):
        dist = jnp.abs(rel + case * half_window)
        for h, slope in enumerate(slopes):
            bias_ref[case, h] = jnp.where(dist <= half_window, -slope * dist.astype(F32), NEG_INF)


Q_SCALE = HEAD_DIM ** -0.5


def _head_scores(qh, kh, bias):
    return lax.dot_general(qh, kh, (((1,), (1,)), ((), ())), preferred_element_type=F32) + bias


def _attn_a_shapes(n, dil):
    tq = min(ATTN_Q_TILE, n)
    res = max(1, min(dil, ATTN_A_ROWS // n))
    return tq, min(n, tq + 2 * A_HALF_WINDOW), res, min(ATTN_A_BLOCKS, res * (n // tq))


def _attn_a_kernel(x_ref, o_ref, lse_ref, s_ref, p_ref, bias_ref, *, n, dil, slopes):
    tq, kw, res, nb = _attn_a_shapes(n, dil)
    per_res = n // tq
    lane = lax.broadcasted_iota(jnp.int32, (tq, LANES), 1)
    lane_head = lax.broadcasted_iota(jnp.int32, (tq, A_WIDTH), 1) // HEAD_DIM
    n_pairs = nb * A_HEADS

    @pl.when((pl.program_id(0) == 0) & (pl.program_id(1) == 0))
    def _():
        _fill_bias(bias_ref, slopes, tq, kw, A_HALF_WINDOW)

    def step(it, carry):
        blocks = []
        for j in range(nb):
            blk = it * nb + j
            r = blk // per_res
            q0 = pl.multiple_of((blk % per_res) * tq, tq)
            start, case = _band(q0, n, kw, A_HALF_WINDOW)
            blocks.append((r, q0, case, x_ref[0, r, pl.ds(q0, tq), 0:A_WIDTH] * Q_SCALE,
                           x_ref[0, r, pl.ds(start, kw), A_WIDTH:2 * A_WIDTH],
                           x_ref[0, r, pl.ds(start, kw), 2 * A_WIDTH:3 * A_WIDTH]))
        for j, (_, _, case, q, k, _) in enumerate(blocks):
            q_heads = jnp.concatenate([jnp.where(lane_head == h, q, jnp.zeros_like(q)) for h in range(A_HEADS)],
                                      axis=0)
            s_all = lax.dot_general(q_heads, k, (((1,), (1,)), ((), ())), preferred_element_type=F32)
            for h in range(A_HEADS):
                s_ref[j * A_HEADS + h] = s_all[h * tq:(h + 1) * tq] + bias_ref[case, h]
        ms = [jnp.max(s_ref[i], axis=-1, keepdims=True) for i in range(n_pairs)]
        ls = []
        for i in range(n_pairs):
            p = jnp.exp(s_ref[i] - ms[i])
            ls.append(jnp.sum(p, axis=-1, keepdims=True))
            p_ref[i] = p.astype(BF16)
        for j in range(nb):
            r, q0, v = blocks[j][0], blocks[j][1], blocks[j][5]
            pv = _dot(p_ref[pl.ds(j * A_HEADS, A_HEADS)].reshape(A_HEADS * tq, kw), v)
            out = jnp.zeros((tq, A_WIDTH), F32)
            lse_tile = jnp.zeros((tq, LANES), F32)
            for h in range(A_HEADS):
                i = j * A_HEADS + h
                out = jnp.where(lane_head == h, pv[h * tq:(h + 1) * tq] / ls[i], out)
                lse_tile = jnp.where(lane == h, ms[i] + jnp.log(ls[i]), lse_tile)
            o_ref[0, r, pl.ds(q0, tq), :] = out.astype(BF16)
            lse_ref[0, r, pl.ds(q0, tq), :] = lse_tile
        return carry

    lax.fori_loop(0, res * per_res // nb, step, 0)


def _attn_a(pa, gi):
    window, dil = A_GROUPS[gi]
    assert window // (2 * dil) == A_HALF_WINDOW
    bsz, _, n, gw = pa.shape
    tq, kw, res, nb = _attn_a_shapes(n, dil)
    assert dil % res == 0 and (res * (n // tq)) % nb == 0
    slopes = _alibi_slopes(len(A_GROUPS) * A_HEADS).reshape(len(A_GROUPS), A_HEADS)[gi]
    slopes = tuple(float(s * np.float32(dil)) for s in slopes)
    block = lambda w: pl.BlockSpec((1, res, n, w), lambda b, r: (b, r, 0, 0))
    return pl.pallas_call(
        functools.partial(_attn_a_kernel, n=n, dil=dil, slopes=slopes),
        grid=(bsz, dil // res),
        in_specs=[block(gw)],
        out_specs=[block(A_WIDTH), block(LANES)],
        out_shape=[jax.ShapeDtypeStruct((bsz, dil, n, A_WIDTH), BF16),
                   jax.ShapeDtypeStruct((bsz, dil, n, LANES), F32)],
        scratch_shapes=[pltpu.VMEM((nb * A_HEADS, tq, kw), F32), pltpu.VMEM((nb * A_HEADS, tq, kw), BF16),
                        pltpu.VMEM((BAND_CASES, A_HEADS, tq, kw), F32)],
        compiler_params=_params("arbitrary", "arbitrary"),
        name=f"attn_a{gi}",
    )(pa)


def _attn_b_kernel(sink_ref, q_ref, kv_ref, o_ref, s_ref, p_ref, bias_ref, *, n, slopes):
    hw = B_HALF_WINDOW
    tq = min(ATTN_Q_TILE, n)
    kw = min(n, tq + 2 * hw)
    heads = range(B_Q_HEADS)
    low = lax.broadcasted_iota(jnp.int32, (tq, LANES), 1) < HEAD_DIM
    swap = lambda t: jnp.concatenate([t[:, HEAD_DIM:], t[:, :HEAD_DIM]], axis=1)

    @pl.when(pl.program_id(0) == 0)
    def _():
        _fill_bias(bias_ref, slopes, tq, kw, hw)

    nb = min(ATTN_B_BLOCKS, n // tq)
    pairs = [(j, hq) for j in range(nb) for hq in heads]

    def step(it, carry):
        blocks = []
        for j in range(nb):
            q0 = pl.multiple_of((it * nb + j) * tq, tq)
            start, case = _band(q0, n, kw, hw)
            blocks.append((q0, case, q_ref[0, pl.ds(q0, tq), :] * Q_SCALE,
                           kv_ref[0, pl.ds(start, kw), 0:B_KV_WIDTH],
                           kv_ref[0, pl.ds(start, kw), B_KV_WIDTH:2 * B_KV_WIDTH]))
        for j, (_, case, q, k2, _) in enumerate(blocks):
            for hk in range(B_KV_HEADS):
                rows = []
                for hq in range(hk * B_GROUP, (hk + 1) * B_GROUP):
                    tile = q[:, (hq // 2) * LANES:(hq // 2 + 1) * LANES]
                    moved = tile if hq % 2 == hk else swap(tile)
                    rows.append(jnp.where(low == (hk == 0), moved, jnp.zeros_like(moved)))
                s_all = lax.dot_general(jnp.concatenate(rows, axis=0), k2, (((1,), (1,)), ((), ())),
                                        preferred_element_type=F32)
                for g in range(B_GROUP):
                    hq = hk * B_GROUP + g
                    s_ref[j * B_Q_HEADS + hq] = s_all[g * tq:(g + 1) * tq] + bias_ref[case, hq]
        ms = [jnp.maximum(jnp.max(s_ref[i], axis=-1, keepdims=True), sink_ref[hq]) for i, (_, hq) in enumerate(pairs)]
        ls = []
        for i, (_, hq) in enumerate(pairs):
            p = jnp.exp(s_ref[i] - ms[i])
            ls.append(jnp.sum(p, axis=-1, keepdims=True) + jnp.exp(sink_ref[hq] - ms[i]))
            p_ref[i] = p.astype(BF16)
        for j, (q0, _, _, _, v2) in enumerate(blocks):
            placed = []
            for hk in range(B_KV_HEADS):
                first = j * B_Q_HEADS + hk * B_GROUP
                pv = _dot(p_ref[pl.ds(first, B_GROUP)].reshape(B_GROUP * tq, kw), v2)
                for g in range(B_GROUP):
                    hq = hk * B_GROUP + g
                    o = pv[g * tq:(g + 1) * tq] / ls[first + g]
                    placed.append(o if hq % 2 == hk else swap(o))
            tiles = [jnp.where(low, placed[t], placed[t + 1]) for t in range(0, B_Q_HEADS, 2)]
            o_ref[0, pl.ds(q0, tq), :] = jnp.concatenate(tiles, axis=1).astype(BF16)
        return carry

    lax.fori_loop(0, n // (tq * nb), step, 0)


def _attn_b(qb, kvb, sink, bsz, seq):
    slopes = tuple(float(s) for s in _alibi_slopes(B_Q_HEADS))
    tq = min(ATTN_Q_TILE, seq)
    kw = min(seq, tq + 2 * B_HALF_WINDOW)
    nb = min(ATTN_B_BLOCKS, seq // tq)
    assert seq % (tq * nb) == 0
    out = pl.pallas_call(
        functools.partial(_attn_b_kernel, n=seq, slopes=slopes),
        grid=(bsz,),
        in_specs=[pl.BlockSpec(memory_space=pltpu.SMEM),
                  pl.BlockSpec((1, seq, B_Q_WIDTH), lambda b: (b, 0, 0)),
                  pl.BlockSpec((1, seq, 2 * B_KV_WIDTH), lambda b: (b, 0, 0))],
        out_specs=pl.BlockSpec((1, seq, B_Q_WIDTH), lambda b: (b, 0, 0)),
        out_shape=jax.ShapeDtypeStruct((bsz, seq, B_Q_WIDTH), BF16),
        scratch_shapes=[pltpu.VMEM((nb * B_Q_HEADS, tq, kw), F32), pltpu.VMEM((nb * B_Q_HEADS, tq, kw), BF16),
                        pltpu.VMEM((BAND_CASES, B_Q_HEADS, tq, kw), F32)],
        compiler_params=_params("arbitrary"),
        name="attn_b",
    )(sink, qb.reshape(bsz, seq, B_Q_WIDTH), kvb.reshape(bsz, seq, 2 * B_KV_WIDTH))
    return out.reshape(bsz * seq, B_Q_WIDTH)


def _split_bf16(a):
    hi = a.astype(BF16)
    return hi, (a - hi.astype(F32)).astype(BF16)


def _top2(h2, wr_a, wr_b):
    tm = h2.shape[0]
    nt = (((1,), (1,)), ((), ()))
    h_hi, h_lo = _split_bf16(h2)
    by_hi = lax.dot_general(wr_a, h_hi, nt, preferred_element_type=F32)
    by_lo = lax.dot_general(wr_b, h_lo, nt, preferred_element_type=F32)
    logits = by_hi[:N_EXPERTS] + (by_hi[N_EXPERTS:] + by_lo[:N_EXPERTS])
    row = lax.broadcasted_iota(jnp.int32, logits.shape, 0)
    m1 = jnp.max(logits, axis=0, keepdims=True)
    i1 = jnp.min(jnp.where(logits == m1, row, N_EXPERTS), axis=0, keepdims=True)
    l2 = jnp.where(row == i1, -jnp.inf, logits)
    m2 = jnp.max(l2, axis=0, keepdims=True)
    i2 = jnp.min(jnp.where(l2 == m2, row, N_EXPERTS), axis=0, keepdims=True)
    e2 = jnp.exp(m2 - m1)
    den = 1.0 + e2
    packed = jnp.where(row == 0, i1.astype(F32),
                       jnp.where(row == 1, i2.astype(F32),
                                 jnp.where(row == 2, 1.0 / den, jnp.where(row == 3, e2 / den, 0.0))))
    padded = jnp.concatenate([packed, jnp.zeros((LANES - N_EXPERTS, tm), F32)], axis=0)
    return padded.T


def _interleave_rows(dst_ref, tmp_ref, piece, dil):
    rows = TOKEN_TILE // dil
    if dil <= ROW_STRIDE:
        for r in range(dil):
            dst_ref[pl.ds(r, rows, stride=dil), :] = piece(r)
        return
    outer = dil // ROW_STRIDE
    for r in range(dil):
        tmp_ref[r % ROW_STRIDE, pl.ds(r // ROW_STRIDE, rows, stride=outer), :] = piece(r)
    for r1 in range(ROW_STRIDE):
        dst_ref[pl.ds(r1, TOKEN_TILE // ROW_STRIDE, stride=ROW_STRIDE), :] = tmp_ref[r1]


def _merge_kernel(*refs, moe):
    ng = len(A_GROUPS)
    x_ref = refs[0]
    o_refs, l_refs = refs[1:1 + ng], refs[1 + ng:1 + 2 * ng]
    yb_ref, gmix_ref, wgate_ref, bgate_ref, wpa_ref, wpb_ref, wout_ref, gffn_ref = refs[1 + 2 * ng:9 + 2 * ng]
    rest = refs[9 + 2 * ng:]
    if moe:
        wr_hi_ref, wr_lo_ref, xo_ref, h2_ref, rout_ref, so_ref, sl_ref, tmp_ref = rest
    else:
        xo_ref, h2_ref, so_ref, sl_ref, tmp_ref = rest
    x = x_ref[...]
    h = _rms(x, gmix_ref[...]).astype(BF16)
    z_b = _dot(yb_ref[...], wpb_ref[...])
    g_a = jax.nn.sigmoid(_dot(h, wgate_ref[:, :D_MODEL]) + bgate_ref[:, :D_MODEL])
    g_b = jax.nn.sigmoid(_dot(h, wgate_ref[:, D_MODEL:]) + bgate_ref[:, D_MODEL:])
    outs, lses = [], []
    for gi, (_, dil) in enumerate(A_GROUPS):
        if dil == 1:
            outs.append(o_refs[gi][0, 0].astype(F32))
            lses.append(l_refs[gi][0, 0])
        else:
            chunks = A_WIDTH // LANES
            for c in range(chunks):
                piece = lambda r, c=c, gi=gi: o_refs[gi][0, r, :, c * LANES:(c + 1) * LANES].astype(F32)
                _interleave_rows(so_ref.at[gi, c], tmp_ref.at[c], piece, dil)
            _interleave_rows(sl_ref.at[gi], tmp_ref.at[chunks], lambda r, gi=gi: l_refs[gi][0, r], dil)
            outs.append(jnp.concatenate([so_ref[gi, c] for c in range(chunks)], axis=-1))
            lses.append(sl_ref[gi])
    mx = jnp.maximum(jnp.maximum(lses[0], lses[1]), lses[2])
    es = [jnp.exp(l - mx) for l in lses]
    den = es[0] + es[1] + es[2]
    heads = []
    for hh in range(A_HEADS):
        cols = slice(hh * HEAD_DIM, (hh + 1) * HEAD_DIM)
        acc = None
        for g in range(ng):
            term = (es[g][:, hh:hh + 1] / den[:, hh:hh + 1]) * outs[g][:, cols]
            acc = term if acc is None else acc + term
        heads.append(acc)
    y_a = jnp.concatenate(heads, axis=-1).astype(BF16)
    merged = g_a * _dot(y_a, wpa_ref[...]) + g_b * z_b
    xn = x + _dot(merged.astype(BF16), wout_ref[...])
    xo_ref[...] = xn
    h2 = _rms(xn, gffn_ref[...])
    h2_ref[...] = h2.astype(h2_ref.dtype)
    if moe:
        rout_ref[...] = _top2(h2, wr_hi_ref[...], wr_lo_ref[...])


def _merge(x, outs, lses, y_b, seq, gmix, w_gate, b_gate, w_pa, w_pb, w_out, gffn, w_router=None):
    t = x.shape[0]
    tm = TOKEN_TILE
    moe = w_router is not None
    row = lambda i: (i, 0)
    fixed = lambda i: (0, 0)
    full = lambda a: pl.BlockSpec(a.shape, fixed)
    weights = [gmix, w_gate, b_gate, w_pa, w_pb, w_out, gffn]
    out_specs = [pl.BlockSpec((tm, D_MODEL), row), pl.BlockSpec((tm, D_MODEL), row)]
    out_shape = [jax.ShapeDtypeStruct((t, D_MODEL), F32), jax.ShapeDtypeStruct((t, D_MODEL), F32 if moe else BF16)]
    if moe:
        wr_hi, wr_lo = _split_bf16(w_router.T)
        weights.extend([jnp.concatenate([wr_hi, wr_lo]), jnp.concatenate([wr_hi, jnp.zeros_like(wr_hi)])])
        out_specs.append(pl.BlockSpec((tm, LANES), row))
        out_shape.append(jax.ShapeDtypeStruct((t, LANES), F32))
    ng = len(A_GROUPS)
    return pl.pallas_call(
        functools.partial(_merge_kernel, moe=moe),
        grid=(t // tm,),
        in_specs=[pl.BlockSpec((tm, D_MODEL), row)]
                 + [_residue_spec(seq, dil, A_WIDTH) for _, dil in A_GROUPS]
                 + [_residue_spec(seq, dil, LANES) for _, dil in A_GROUPS]
                 + [pl.BlockSpec((tm, B_Q_WIDTH), row)] + [full(a) for a in weights],
        out_specs=out_specs,
        out_shape=out_shape,
        scratch_shapes=[pltpu.VMEM((ng, A_WIDTH // LANES, tm, LANES), F32), pltpu.VMEM((ng, tm, LANES), F32),
                        pltpu.VMEM((A_WIDTH // LANES + 1, ROW_STRIDE, tm // ROW_STRIDE, LANES), F32)],
        compiler_params=_params("parallel"),
        name="merge",
    )(x, *outs, *lses, y_b, *weights)


def _swiglu_step(xb, wg, wu, wd):
    a = _dot(xb, wg)
    return _dot((jax.nn.silu(a) * _dot(xb, wu)).astype(BF16), wd)


def _ffn_kernel(h_ref, x_ref, wg_ref, wu_ref, wd_ref, o_ref, hm_ref):
    h = h_ref[...]
    d_ff = wg_ref.shape[1]
    for lo in range(0, d_ff, FF_CHUNK_DENSE):
        cols = slice(lo, min(lo + FF_CHUNK_DENSE, d_ff))
        hm_ref[:, cols] = (jax.nn.silu(_dot(h, wg_ref[:, cols])) * _dot(h, wu_ref[:, cols])).astype(BF16)
    o_ref[...] = x_ref[...] + _dot(hm_ref[...], wd_ref[...])


def _ffn(h2, x, w_g, w_u, w_d):
    t = x.shape[0]
    tm = TOKEN_TILE
    d_ff = w_g.shape[1]
    row = lambda i: (i, 0)
    fixed = lambda i: (0, 0)
    return pl.pallas_call(
        _ffn_kernel,
        grid=(t // tm,),
        in_specs=[pl.BlockSpec((tm, D_MODEL), row), pl.BlockSpec((tm, D_MODEL), row),
                  pl.BlockSpec((D_MODEL, d_ff), fixed), pl.BlockSpec((D_MODEL, d_ff), fixed),
                  pl.BlockSpec((d_ff, D_MODEL), fixed)],
        out_specs=pl.BlockSpec((tm, D_MODEL), row),
        out_shape=jax.ShapeDtypeStruct((t, D_MODEL), F32),
        scratch_shapes=[pltpu.VMEM((tm, d_ff), BF16)],
        compiler_params=_params("parallel"),
        name="ffn_dense",
    )(h2, x, w_g, w_u, w_d)


def _dispatch_kernel(zero_ref, dest_ref, h_hbm, xs_ref, hbuf_ref, zbuf_ref, in_sem, out_sem, zsem):
    tm = hbuf_ref.shape[1]
    i = pl.program_id(0)
    steps = pl.num_programs(0)
    load = lambda step, slot: pltpu.make_async_copy(
        h_hbm.at[pl.ds(pl.multiple_of(step * tm, tm), tm), :], hbuf_ref.at[slot], in_sem.at[slot])

    def wait_rows(slot):
        for _ in range(TOP_K):
            pltpu.make_async_copy(hbuf_ref.at[slot], xs_ref.at[pl.ds(0, tm), :], out_sem.at[slot]).wait()

    @pl.when(i == 0)
    def _():
        load(0, 0).start()
        zbuf_ref[...] = jnp.zeros_like(zbuf_ref)
        n_zero = zero_ref.shape[0]
        zero_copy = lambda z: pltpu.make_async_copy(
            zbuf_ref, xs_ref.at[pl.ds(pl.multiple_of(zero_ref[z], MOE_TILE), MOE_TILE), :], zsem)
        for z in range(n_zero):
            zero_copy(z).start()
            zero_copy(z).wait()

    for slot in range(2):
        @pl.when(i % 2 == slot)
        def _():
            load(i, slot).wait()

            @pl.when(i >= 1)
            def _():
                wait_rows(1 - slot)

            @pl.when(i + 1 < steps)
            def _():
                load(i + 1, 1 - slot).start()

            for r in range(tm):
                for k in range(TOP_K):
                    d = dest_ref[0, 0, TOP_K * r + k]
                    pltpu.make_async_copy(hbuf_ref.at[slot, pl.ds(r, 1), :], xs_ref.at[pl.ds(d, 1), :],
                                          out_sem.at[slot]).start(priority=k)

            @pl.when(i == steps - 1)
            def _():
                wait_rows(slot)


def _dispatch(h2, dest, zero_blocks, cap):
    t = h2.shape[0]
    tm = DISPATCH_TILE
    return pl.pallas_call(
        _dispatch_kernel,
        grid_spec=pltpu.PrefetchScalarGridSpec(
            num_scalar_prefetch=1,
            grid=(t // tm,),
            in_specs=[pl.BlockSpec((1, 1, TOP_K * tm), lambda i, z: (i, 0, 0), memory_space=pltpu.SMEM),
                      pl.BlockSpec(memory_space=pl.ANY)],
            out_specs=pl.BlockSpec(memory_space=pl.ANY),
            scratch_shapes=[pltpu.VMEM((2, tm, D_MODEL), F32), pltpu.VMEM((MOE_TILE, D_MODEL), F32),
                            pltpu.SemaphoreType.DMA((2,)), pltpu.SemaphoreType.DMA((2,)),
                            pltpu.SemaphoreType.DMA(())],
        ),
        out_shape=jax.ShapeDtypeStruct((cap, D_MODEL), F32),
        compiler_params=_params("arbitrary"),
        name="dispatch",
    )(zero_blocks, dest.reshape(t // tm, 1, TOP_K * tm), h2)


def _expert_kernel(be_ref, used_ref, xs_ref, wg_ref, wu_ref, wd_ref, ys_ref, xb_ref, hm_ref):
    del be_ref
    i, j = pl.program_id(0), pl.program_id(1)
    last = pl.num_programs(1) - 1
    tf = wg_ref.shape[2]
    active = i < used_ref[0]

    @pl.when(active & (j == 0))
    def _():
        xb_ref[...] = xs_ref[...].astype(BF16)

    for c in range(FF_SPLIT_EXPERT):
        @pl.when(active & (j == c))
        def _():
            xb = xb_ref[...]
            hm_ref[:, c * tf:(c + 1) * tf] = (jax.nn.silu(_dot(xb, wg_ref[0])) * _dot(xb, wu_ref[0])).astype(BF16)

    @pl.when(active & (j == last))
    def _():
        ys_ref[...] = _dot(hm_ref[...], wd_ref[0])

    @pl.when(jnp.logical_not(active) & (j == last))
    def _():
        ys_ref[...] = jnp.zeros_like(ys_ref)


def _experts(xs, block_e, n_used, w_g, w_u, w_d):
    cap = xs.shape[0]
    tm = MOE_TILE
    d_ff = w_g.shape[2]
    tf = d_ff // FF_SPLIT_EXPERT
    assert tf * FF_SPLIT_EXPERT == d_ff and tf % LANES == 0
    chunk = lambda i, j, used: jnp.where(i < used[0], j, FF_SPLIT_EXPERT - 1)
    return pl.pallas_call(
        _expert_kernel,
        grid_spec=pltpu.PrefetchScalarGridSpec(
            num_scalar_prefetch=2,
            grid=(cap // tm, FF_SPLIT_EXPERT),
            in_specs=[pl.BlockSpec((tm, D_MODEL), lambda i, j, be, used: (i, 0)),
                      pl.BlockSpec((1, D_MODEL, tf), lambda i, j, be, used: (be[i], 0, chunk(i, j, used))),
                      pl.BlockSpec((1, D_MODEL, tf), lambda i, j, be, used: (be[i], 0, chunk(i, j, used))),
                      pl.BlockSpec((1, d_ff, D_MODEL), lambda i, j, be, used: (be[i], 0, 0))],
            out_specs=pl.BlockSpec((tm, D_MODEL), lambda i, j, be, used: (i, 0)),
            scratch_shapes=[pltpu.VMEM((tm, D_MODEL), BF16), pltpu.VMEM((tm, d_ff), BF16)],
        ),
        out_shape=jax.ShapeDtypeStruct((cap, D_MODEL), F32),
        compiler_params=_params("arbitrary", "arbitrary"),
        name="experts",
    )(block_e, n_used, xs, w_g, w_u, w_d)


def _combine_kernel(dest_ref, dest_next_ref, r_ref, x_ref, ys_ref, g_ref, o_ref, y_ref, sem):
    tm = x_ref.shape[0]
    i = pl.program_id(0)

    def fetch(idx_ref, slot):
        for r in range(tm):
            for k in range(TOP_K):
                d = idx_ref[0, 0, TOP_K * r + k]
                pltpu.make_async_copy(ys_ref.at[pl.ds(d, 1), :], y_ref.at[slot, k, pl.ds(r, 1), :],
                                      sem.at[slot, k]).start(priority=k)

    @pl.when(i == 0)
    def _():
        fetch(dest_ref, 0)

    for parity in range(2):
        @pl.when((i + 1 < pl.num_programs(0)) & ((i + 1) % 2 == parity))
        def _():
            fetch(dest_next_ref, parity)

    slot = i % 2
    for k in range(TOP_K):
        pltpu.make_async_copy(ys_ref.at[pl.ds(0, tm), :], y_ref.at[slot, k], sem.at[slot, k]).wait()
    gates = r_ref[...]
    moe = gates[:, 2:3] * y_ref[slot, 0] + gates[:, 3:4] * y_ref[slot, 1]
    o_ref[...] = _rms(x_ref[...] + moe, g_ref[...])


def _combine(ys, dest, rout, x, g_final):
    t = x.shape[0]
    tm = COMBINE_TILE
    steps = t // tm
    dest3 = dest.reshape(steps, 1, TOP_K * tm)
    return pl.pallas_call(
        _combine_kernel,
        grid=(steps,),
        in_specs=[pl.BlockSpec((1, 1, TOP_K * tm), lambda i: (i, 0, 0), memory_space=pltpu.SMEM),
                  pl.BlockSpec((1, 1, TOP_K * tm), lambda i: (jnp.minimum(i + 1, steps - 1), 0, 0),
                               memory_space=pltpu.SMEM),
                  pl.BlockSpec((tm, LANES), lambda i: (i, 0)),
                  pl.BlockSpec((tm, D_MODEL), lambda i: (i, 0)),
                  pl.BlockSpec(memory_space=pl.ANY),
                  pl.BlockSpec((1, D_MODEL), lambda i: (0, 0))],
        out_specs=pl.BlockSpec((tm, D_MODEL), lambda i: (i, 0)),
        out_shape=jax.ShapeDtypeStruct((t, D_MODEL), F32),
        scratch_shapes=[pltpu.VMEM((2, TOP_K, tm, D_MODEL), F32), pltpu.SemaphoreType.DMA((2, TOP_K))],
        compiler_params=_params("arbitrary"),
        name="combine",
    )(dest3, dest3, rout, x, ys, g_final)


def _routing_tables(rout, t):
    tm = MOE_TILE
    e_flat = rout[:, :TOP_K].astype(jnp.int32).reshape(-1)
    onehot = (e_flat[:, None] == jnp.arange(N_EXPERTS, dtype=jnp.int32)[None, :]).astype(jnp.int32)
    csum = jnp.cumsum(onehot, axis=0)
    counts = csum[-1]
    padded = (counts + tm - 1) // tm * tm
    pend = jnp.cumsum(padded)
    pstart = pend - padded
    dest = jnp.sum(onehot * (csum - 1 + pstart[None, :]), axis=1).astype(jnp.int32)
    n_blocks = -(-(t * TOP_K) // tm) + N_EXPERTS
    block_start = jnp.arange(n_blocks, dtype=jnp.int32) * tm
    block_e = jnp.minimum(jnp.sum((pend[None, :] <= block_start[:, None]).astype(jnp.int32), axis=1), N_EXPERTS - 1)
    zero_blocks = jnp.concatenate([jnp.maximum(pend - tm, 0), block_start[n_blocks - N_EXPERTS:]]).astype(jnp.int32)
    n_used = (pend[N_EXPERTS - 1:] // tm).astype(jnp.int32)
    return dest, block_e, n_used, zero_blocks, n_blocks * tm


def _trunk(x3, wts):
    bsz, seq, _ = x3.shape
    assert seq % TOKEN_TILE == 0
    x = x3.reshape(bsz * seq, D_MODEL)
    depth = len(wts["layers"])
    for li, lw in enumerate(wts["layers"]):
        *pas, qb, kvb = _proj(x, lw["norm_mix_g"], lw["w_in"], bsz, seq)
        outs, lses = zip(*[_attn_a(pa, gi) for gi, pa in enumerate(pas)])
        y_b = _attn_b(qb, kvb, lw["sink"], bsz, seq)
        mixer = (lw["norm_mix_g"], lw["w_gate"], lw["b_gate"], lw["w_proj_a"], lw["w_proj_b"], lw["w_out"],
                 lw["norm_ffn_g"])
        if "w_router" not in lw:
            assert li < depth - 1
            x, h2 = _merge(x, outs, lses, y_b, seq, *mixer)
            x = _ffn(h2, x, lw["w_ff_gate"], lw["w_ff_up"], lw["w_ff_down"])
        else:
            assert li == depth - 1
            x, h2, rout = _merge(x, outs, lses, y_b, seq, *mixer, w_router=lw["w_router"])
            dest, block_e, n_used, zero_blocks, cap = _routing_tables(rout, bsz * seq)
            xs = _dispatch(h2, dest, zero_blocks, cap)
            ys = _experts(xs, block_e, n_used, lw["w_e_gate"], lw["w_e_up"], lw["w_e_down"])
            x = _combine(ys, dest, rout, x, wts["norm_final_g"])
    return x.reshape(bsz, seq, D_MODEL)


def kernel(x_prompt, x_sample, norm_mix_g, w_in, w_gate, b_gate, w_proj_a, w_proj_b, w_out, sink, norm_ffn_g,
           w_ff_gate, w_ff_up, w_ff_down, w_router, w_e_gate, w_e_up, w_e_down, norm_final_g):
    depth = w_in.shape[0]
    layers = []
    for li in range(depth):
        lw = {
            "norm_mix_g": norm_mix_g[li].reshape(1, D_MODEL), "w_in": w_in[li].astype(BF16),
            "w_gate": w_gate[li].astype(BF16), "b_gate": b_gate[li].reshape(1, 2 * D_MODEL),
            "w_proj_a": w_proj_a[li].astype(BF16), "w_proj_b": w_proj_b[li].astype(BF16),
            "w_out": w_out[li].astype(BF16), "sink": sink[li], "norm_ffn_g": norm_ffn_g[li].reshape(1, D_MODEL),
        }
        i = li // 2
        if li % 2 == 0:
            lw.update(w_ff_gate=w_ff_gate[i].astype(BF16), w_ff_up=w_ff_up[i].astype(BF16),
                      w_ff_down=w_ff_down[i].astype(BF16))
        else:
            lw.update(w_router=w_router[i], w_e_gate=w_e_gate[i].astype(BF16), w_e_up=w_e_up[i].astype(BF16),
                      w_e_down=w_e_down[i].astype(BF16))
        layers.append(lw)
    wts = {"layers": layers, "norm_final_g": norm_final_g.reshape(1, D_MODEL)}
    return _trunk(x_prompt, wts), _trunk(x_sample, wts)
```

```python
import functools

import numpy as np
import jax
import jax.numpy as jnp
from jax import lax
from jax.experimental import pallas as pl
from jax.experimental.pallas import tpu as pltpu

F32 = jnp.float32
BF16 = jnp.bfloat16

D_MODEL = 1024
HEAD_DIM = 64
A_GROUPS = ((128, 1), (512, 4), (2048, 16))
A_HALF_WINDOW = 64
A_HEADS = 4
A_WIDTH = A_HEADS * HEAD_DIM
A_COLS = len(A_GROUPS) * 3 * A_WIDTH
B_Q_HEADS = 8
B_KV_HEADS = 2
B_GROUP = B_Q_HEADS // B_KV_HEADS
B_Q_WIDTH = B_Q_HEADS * HEAD_DIM
B_KV_WIDTH = B_KV_HEADS * HEAD_DIM
B_HALF_WINDOW = 128
N_IN = A_COLS + B_Q_WIDTH + 2 * B_KV_WIDTH
N_EXPERTS = 8
TOP_K = 2
RMS_EPS = 1e-6
NEG_INF = -1e30

LANES = 128
TOKEN_TILE = 512
ATTN_Q_TILE = 128
ATTN_A_ROWS = 2048
ATTN_A_BLOCKS = 4
ATTN_B_BLOCKS = 2
FF_CHUNK_DENSE = 1536
FF_SPLIT_EXPERT = 2
MOE_TILE = 512
DISPATCH_TILE = 512
COMBINE_TILE = 512
ROW_STRIDE = 4
VMEM_LIMIT = 52 * 1024 * 1024


def _alibi_slopes(n):
    return np.asarray(2.0 ** (-8.0 * np.arange(1, n + 1) / n), dtype=np.float32)


def _params(*sem):
    return pltpu.CompilerParams(dimension_semantics=sem, vmem_limit_bytes=VMEM_LIMIT)


def _rms(x, g):
    return x * lax.rsqrt(jnp.mean(x * x, axis=-1, keepdims=True) + RMS_EPS) * g


def _dot(a, b):
    return jnp.dot(a, b, preferred_element_type=F32)


def _residue_spec(seq, dil, width):
    tiles_per_seq = seq // TOKEN_TILE
    return pl.BlockSpec((1, dil, TOKEN_TILE // dil, width),
                        lambda i: (i // tiles_per_seq, 0, i % tiles_per_seq, 0))


def _proj_kernel(x_ref, g_ref, w_ref, *refs):
    pa_refs, (qb_ref, kv_ref, scr_ref, tmp_ref) = refs[:len(A_GROUPS)], refs[len(A_GROUPS):]
    h = _rms(x_ref[...], g_ref[...]).astype(BF16)
    for gi, (_, dil) in enumerate(A_GROUPS):
        cols = slice(gi * 3 * A_WIDTH, (gi + 1) * 3 * A_WIDTH)
        res = _dot(h, w_ref[:, cols])
        if dil == 1:
            pa_refs[gi][0, 0] = res.astype(BF16)
        else:
            chunks = res.shape[1] // LANES
            for c in range(chunks):
                scr_ref[c] = res[:, c * LANES:(c + 1) * LANES]
            rows = TOKEN_TILE // dil
            if dil <= ROW_STRIDE:
                src = lambda c, r: scr_ref[c, pl.ds(r, rows, stride=dil), :]
            else:
                outer = dil // ROW_STRIDE
                for c in range(chunks):
                    for r1 in range(ROW_STRIDE):
                        tmp_ref[c, r1] = scr_ref[c, pl.ds(r1, TOKEN_TILE // ROW_STRIDE, stride=ROW_STRIDE), :]
                src = lambda c, r: tmp_ref[c, r % ROW_STRIDE, pl.ds(r // ROW_STRIDE, rows, stride=outer), :]
            for r in range(dil):
                parts = [src(c, r) for c in range(chunks)]
                pa_refs[gi][0, r] = jnp.concatenate(parts, axis=-1).astype(BF16)
    qb_ref[...] = _dot(h, w_ref[:, A_COLS:A_COLS + B_Q_WIDTH]).astype(BF16)
    kv_ref[...] = _dot(h, w_ref[:, A_COLS + B_Q_WIDTH:]).astype(BF16)


def _proj(x, g, w_in, bsz, seq):
    t = x.shape[0]
    tm = TOKEN_TILE
    row = lambda i: (i, 0)
    fixed = lambda i: (0, 0)
    gw = 3 * A_WIDTH
    return pl.pallas_call(
        _proj_kernel,
        grid=(t // tm,),
        in_specs=[pl.BlockSpec((tm, D_MODEL), row), pl.BlockSpec((1, D_MODEL), fixed),
                  pl.BlockSpec((D_MODEL, N_IN), fixed)],
        out_specs=[_residue_spec(seq, dil, gw) for _, dil in A_GROUPS]
                  + [pl.BlockSpec((tm, B_Q_WIDTH), row), pl.BlockSpec((tm, 2 * B_KV_WIDTH), row)],
        out_shape=[jax.ShapeDtypeStruct((bsz, dil, seq // dil, gw), BF16) for _, dil in A_GROUPS]
                  + [jax.ShapeDtypeStruct((t, B_Q_WIDTH), BF16), jax.ShapeDtypeStruct((t, 2 * B_KV_WIDTH), BF16)],
        scratch_shapes=[pltpu.VMEM((gw // LANES, tm, LANES), F32),
                        pltpu.VMEM((gw // LANES, ROW_STRIDE, tm // ROW_STRIDE, LANES), F32)],
        compiler_params=_params("parallel"),
        name="proj",
    )(x, g, w_in)


BAND_CASES = 3


def _band(q0, n, kw, half_window):
    start = pl.multiple_of(jnp.clip(q0 - half_window, 0, n - kw), half_window)
    return start, (q0 - start) // half_window


def _fill_bias(bias_ref, slopes, tq, kw, half_window):
    rel = lax.broadcasted_iota(jnp.int32, (tq, kw), 0) - lax.broadcasted_iota(jnp.int32, (tq, kw), 1)
    for case in range(BAND_CASES):
        dist = jnp.abs(rel + case * half_window)
        for h, slope in enumerate(slopes):
            bias_ref[case, h] = jnp.where(dist <= half_window, -slope * dist.astype(F32), NEG_INF)


Q_SCALE = HEAD_DIM ** -0.5


def _attn_a_shapes(n, dil):
    tq = min(ATTN_Q_TILE, n)
    res = max(1, min(dil, ATTN_A_ROWS // n))
    return tq, min(n, tq + 2 * A_HALF_WINDOW), res, min(ATTN_A_BLOCKS, res * (n // tq))


def _attn_a_kernel(x_ref, o_ref, lse_ref, s_ref, p_ref, bias_ref, *, n, dil, slopes):
    tq, kw, res, nb = _attn_a_shapes(n, dil)
    per_res = n // tq
    lane = lax.broadcasted_iota(jnp.int32, (tq, LANES), 1)
    lane_head = lax.broadcasted_iota(jnp.int32, (tq, A_WIDTH), 1) // HEAD_DIM
    n_pairs = nb * A_HEADS

    @pl.when((pl.program_id(0) == 0) & (pl.program_id(1) == 0))
    def _():
        _fill_bias(bias_ref, slopes, tq, kw, A_HALF_WINDOW)

    def step(it, carry):
        blocks = []
        for j in range(nb):
            blk = it * nb + j
            r = blk // per_res
            q0 = pl.multiple_of((blk % per_res) * tq, tq)
            start, case = _band(q0, n, kw, A_HALF_WINDOW)
            blocks.append((r, q0, case, x_ref[0, r, pl.ds(q0, tq), 0:A_WIDTH] * Q_SCALE,
                           x_ref[0, r, pl.ds(start, kw), A_WIDTH:2 * A_WIDTH],
                           x_ref[0, r, pl.ds(start, kw), 2 * A_WIDTH:3 * A_WIDTH]))
        for j, (_, _, case, q, k, _) in enumerate(blocks):
            q_heads = jnp.concatenate([jnp.where(lane_head == h, q, jnp.zeros_like(q)) for h in range(A_HEADS)],
                                      axis=0)
            s_all = lax.dot_general(q_heads, k, (((1,), (1,)), ((), ())), preferred_element_type=F32)
            for h in range(A_HEADS):
                s_ref[j * A_HEADS + h] = s_all[h * tq:(h + 1) * tq] + bias_ref[case, h]
        ms = [jnp.max(s_ref[i], axis=-1, keepdims=True) for i in range(n_pairs)]
        ls = []
        for i in range(n_pairs):
            p = jnp.exp(s_ref[i] - ms[i])
            ls.append(jnp.sum(p, axis=-1, keepdims=True))
            p_ref[i] = p.astype(BF16)
        for j in range(nb):
            r, q0, v = blocks[j][0], blocks[j][1], blocks[j][5]
            pv = _dot(p_ref[pl.ds(j * A_HEADS, A_HEADS)].reshape(A_HEADS * tq, kw), v)
            out = jnp.zeros((tq, A_WIDTH), F32)
            lse_tile = jnp.zeros((tq, LANES), F32)
            for h in range(A_HEADS):
                i = j * A_HEADS + h
                out = jnp.where(lane_head == h, pv[h * tq:(h + 1) * tq] / ls[i], out)
                lse_tile = jnp.where(lane == h, ms[i] + jnp.log(ls[i]), lse_tile)
            o_ref[0, r, pl.ds(q0, tq), :] = out.astype(BF16)
            lse_ref[0, r, pl.ds(q0, tq), :] = lse_tile
        return carry

    lax.fori_loop(0, res * per_res // nb, step, 0)


def _attn_a(pa, gi):
    window, dil = A_GROUPS[gi]
    assert window // (2 * dil) == A_HALF_WINDOW
    bsz, _, n, gw = pa.shape
    tq, kw, res, nb = _attn_a_shapes(n, dil)
    assert dil % res == 0 and (res * (n // tq)) % nb == 0
    slopes = _alibi_slopes(len(A_GROUPS) * A_HEADS).reshape(len(A_GROUPS), A_HEADS)[gi]
    slopes = tuple(float(s * np.float32(dil)) for s in slopes)
    block = lambda w: pl.BlockSpec((1, res, n, w), lambda b, r: (b, r, 0, 0))
    return pl.pallas_call(
        functools.partial(_attn_a_kernel, n=n, dil=dil, slopes=slopes),
        grid=(bsz, dil // res),
        in_specs=[block(gw)],
        out_specs=[block(A_WIDTH), block(LANES)],
        out_shape=[jax.ShapeDtypeStruct((bsz, dil, n, A_WIDTH), BF16),
                   jax.ShapeDtypeStruct((bsz, dil, n, LANES), F32)],
        scratch_shapes=[pltpu.VMEM((nb * A_HEADS, tq, kw), F32), pltpu.VMEM((nb * A_HEADS, tq, kw), BF16),
                        pltpu.VMEM((BAND_CASES, A_HEADS, tq, kw), F32)],
        compiler_params=_params("arbitrary", "arbitrary"),
        name=f"attn_a{gi}",
    )(pa)


def _attn_b_kernel(sink_ref, q_ref, kv_ref, o_ref, s_ref, p_ref, bias_ref, *, n, slopes):
    hw = B_HALF_WINDOW
    tq = min(ATTN_Q_TILE, n)
    kw = min(n, tq + 2 * hw)
    heads = range(B_Q_HEADS)
    low = lax.broadcasted_iota(jnp.int32, (tq, LANES), 1) < HEAD_DIM
    swap = lambda t: jnp.concatenate([t[:, HEAD_DIM:], t[:, :HEAD_DIM]], axis=1)

    @pl.when(pl.program_id(0) == 0)
    def _():
        _fill_bias(bias_ref, slopes, tq, kw, hw)

    nb = min(ATTN_B_BLOCKS, n // tq)
    pairs = [(j, hq) for j in range(nb) for hq in heads]

    def step(it, carry):
        blocks = []
        for j in range(nb):
            q0 = pl.multiple_of((it * nb + j) * tq, tq)
            start, case = _band(q0, n, kw, hw)
            blocks.append((q0, case, q_ref[0, pl.ds(q0, tq), :] * Q_SCALE,
                           kv_ref[0, pl.ds(start, kw), 0:B_KV_WIDTH],
                           kv_ref[0, pl.ds(start, kw), B_KV_WIDTH:2 * B_KV_WIDTH]))
        for j, (_, case, q, k2, _) in enumerate(blocks):
            for hk in range(B_KV_HEADS):
                rows = []
                for hq in range(hk * B_GROUP, (hk + 1) * B_GROUP):
                    tile = q[:, (hq // 2) * LANES:(hq // 2 + 1) * LANES]
                    moved = tile if hq % 2 == hk else swap(tile)
                    rows.append(jnp.where(low == (hk == 0), moved, jnp.zeros_like(moved)))
                s_all = lax.dot_general(jnp.concatenate(rows, axis=0), k2, (((1,), (1,)), ((), ())),
                                        preferred_element_type=F32)
                for g in range(B_GROUP):
                    hq = hk * B_GROUP + g
                    s_ref[j * B_Q_HEADS + hq] = s_all[g * tq:(g + 1) * tq] + bias_ref[case, hq]
        ms = [jnp.maximum(jnp.max(s_ref[i], axis=-1, keepdims=True), sink_ref[hq]) for i, (_, hq) in enumerate(pairs)]
        ls = []
        for i, (_, hq) in enumerate(pairs):
            p = jnp.exp(s_ref[i] - ms[i])
            ls.append(jnp.sum(p, axis=-1, keepdims=True) + jnp.exp(sink_ref[hq] - ms[i]))
            p_ref[i] = p.astype(BF16)
        for j, (q0, _, _, _, v2) in enumerate(blocks):
            placed = []
            for hk in range(B_KV_HEADS):
                first = j * B_Q_HEADS + hk * B_GROUP
                pv = _dot(p_ref[pl.ds(first, B_GROUP)].reshape(B_GROUP * tq, kw), v2)
                for g in range(B_GROUP):
                    hq = hk * B_GROUP + g
                    o = pv[g * tq:(g + 1) * tq] / ls[first + g]
                    placed.append(o if hq % 2 == hk else swap(o))
            tiles = [jnp.where(low, placed[t], placed[t + 1]) for t in range(0, B_Q_HEADS, 2)]
            o_ref[0, pl.ds(q0, tq), :] = jnp.concatenate(tiles, axis=1).astype(BF16)
        return carry

    lax.fori_loop(0, n // (tq * nb), step, 0)


def _attn_b(qb, kvb, sink, bsz, seq):
    slopes = tuple(float(s) for s in _alibi_slopes(B_Q_HEADS))
    tq = min(ATTN_Q_TILE, seq)
    kw = min(seq, tq + 2 * B_HALF_WINDOW)
    nb = min(ATTN_B_BLOCKS, seq // tq)
    assert seq % (tq * nb) == 0
    out = pl.pallas_call(
        functools.partial(_attn_b_kernel, n=seq, slopes=slopes),
        grid=(bsz,),
        in_specs=[pl.BlockSpec(memory_space=pltpu.SMEM),
                  pl.BlockSpec((1, seq, B_Q_WIDTH), lambda b: (b, 0, 0)),
                  pl.BlockSpec((1, seq, 2 * B_KV_WIDTH), lambda b: (b, 0, 0))],
        out_specs=pl.BlockSpec((1, seq, B_Q_WIDTH), lambda b: (b, 0, 0)),
        out_shape=jax.ShapeDtypeStruct((bsz, seq, B_Q_WIDTH), BF16),
        scratch_shapes=[pltpu.VMEM((nb * B_Q_HEADS, tq, kw), F32), pltpu.VMEM((nb * B_Q_HEADS, tq, kw), BF16),
                        pltpu.VMEM((BAND_CASES, B_Q_HEADS, tq, kw), F32)],
        compiler_params=_params("arbitrary"),
        name="attn_b",
    )(sink, qb.reshape(bsz, seq, B_Q_WIDTH), kvb.reshape(bsz, seq, 2 * B_KV_WIDTH))
    return out.reshape(bsz * seq, B_Q_WIDTH)


def _split_bf16(a):
    hi = a.astype(BF16)
    return hi, (a - hi.astype(F32)).astype(BF16)


def _top2(h2, wr_a, wr_b):
    tm = h2.shape[0]
    nt = (((1,), (1,)), ((), ()))
    h_hi, h_lo = _split_bf16(h2)
    by_hi = lax.dot_general(wr_a, h_hi, nt, preferred_element_type=F32)
    by_lo = lax.dot_general(wr_b, h_lo, nt, preferred_element_type=F32)
    logits = by_hi[:N_EXPERTS] + (by_hi[N_EXPERTS:] + by_lo[:N_EXPERTS])
    row = lax.broadcasted_iota(jnp.int32, logits.shape, 0)
    m1 = jnp.max(logits, axis=0, keepdims=True)
    i1 = jnp.min(jnp.where(logits == m1, row, N_EXPERTS), axis=0, keepdims=True)
    l2 = jnp.where(row == i1, -jnp.inf, logits)
    m2 = jnp.max(l2, axis=0, keepdims=True)
    i2 = jnp.min(jnp.where(l2 == m2, row, N_EXPERTS), axis=0, keepdims=True)
    e2 = jnp.exp(m2 - m1)
    den = 1.0 + e2
    packed = jnp.where(row == 0, i1.astype(F32),
                       jnp.where(row == 1, i2.astype(F32),
                                 jnp.where(row == 2, 1.0 / den, jnp.where(row == 3, e2 / den, 0.0))))
    padded = jnp.concatenate([packed, jnp.zeros((LANES - N_EXPERTS, tm), F32)], axis=0)
    return padded.T


def _interleave_rows(dst_ref, tmp_ref, piece, dil):
    rows = TOKEN_TILE // dil
    if dil <= ROW_STRIDE:
        for r in range(dil):
            dst_ref[pl.ds(r, rows, stride=dil), :] = piece(r)
        return
    outer = dil // ROW_STRIDE
    for r in range(dil):
        tmp_ref[r % ROW_STRIDE, pl.ds(r // ROW_STRIDE, rows, stride=outer), :] = piece(r)
    for r1 in range(ROW_STRIDE):
        dst_ref[pl.ds(r1, TOKEN_TILE // ROW_STRIDE, stride=ROW_STRIDE), :] = tmp_ref[r1]


def _merge_kernel(*refs, moe):
    ng = len(A_GROUPS)
    x_ref = refs[0]
    o_refs, l_refs = refs[1:1 + ng], refs[1 + ng:1 + 2 * ng]
    yb_ref, gmix_ref, wgate_ref, bgate_ref, wpa_ref, wpb_ref, wout_ref, gffn_ref = refs[1 + 2 * ng:9 + 2 * ng]
    rest = refs[9 + 2 * ng:]
    if moe:
        wr_hi_ref, wr_lo_ref, xo_ref, h2_ref, rout_ref, so_ref, sl_ref, tmp_ref = rest
    else:
        xo_ref, h2_ref, so_ref, sl_ref, tmp_ref = rest
    x = x_ref[...]
    h = _rms(x, gmix_ref[...]).astype(BF16)
    z_b = _dot(yb_ref[...], wpb_ref[...])
    g_a = jax.nn.sigmoid(_dot(h, wgate_ref[:, :D_MODEL]) + bgate_ref[:, :D_MODEL])
    g_b = jax.nn.sigmoid(_dot(h, wgate_ref[:, D_MODEL:]) + bgate_ref[:, D_MODEL:])
    outs, lses = [], []
    for gi, (_, dil) in enumerate(A_GROUPS):
        if dil == 1:
            outs.append(o_refs[gi][0, 0].astype(F32))
            lses.append(l_refs[gi][0, 0])
        else:
            chunks = A_WIDTH // LANES
            for c in range(chunks):
                piece = lambda r, c=c, gi=gi: o_refs[gi][0, r, :, c * LANES:(c + 1) * LANES].astype(F32)
                _interleave_rows(so_ref.at[gi, c], tmp_ref.at[c], piece, dil)
            _interleave_rows(sl_ref.at[gi], tmp_ref.at[chunks], lambda r, gi=gi: l_refs[gi][0, r], dil)
            outs.append(jnp.concatenate([so_ref[gi, c] for c in range(chunks)], axis=-1))
            lses.append(sl_ref[gi])
    mx = jnp.maximum(jnp.maximum(lses[0], lses[1]), lses[2])
    es = [jnp.exp(l - mx) for l in lses]
    den = es[0] + es[1] + es[2]
    heads = []
    for hh in range(A_HEADS):
        cols = slice(hh * HEAD_DIM, (hh + 1) * HEAD_DIM)
        acc = None
        for g in range(ng):
            term = (es[g][:, hh:hh + 1] / den[:, hh:hh + 1]) * outs[g][:, cols]
            acc = term if acc is None else acc + term
        heads.append(acc)
    y_a = jnp.concatenate(heads, axis=-1).astype(BF16)
    merged = g_a * _dot(y_a, wpa_ref[...]) + g_b * z_b
    xn = x + _dot(merged.astype(BF16), wout_ref[...])
    xo_ref[...] = xn
    h2 = _rms(xn, gffn_ref[...])
    h2_ref[...] = h2.astype(h2_ref.dtype)
    if moe:
        rout_ref[...] = _top2(h2, wr_hi_ref[...], wr_lo_ref[...])


def _merge(x, outs, lses, y_b, seq, gmix, w_gate, b_gate, w_pa, w_pb, w_out, gffn, w_router=None):
    t = x.shape[0]
    tm = TOKEN_TILE
    moe = w_router is not None
    row = lambda i: (i, 0)
    fixed = lambda i: (0, 0)
    full = lambda a: pl.BlockSpec(a.shape, fixed)
    weights = [gmix, w_gate, b_gate, w_pa, w_pb, w_out, gffn]
    out_specs = [pl.BlockSpec((tm, D_MODEL), row), pl.BlockSpec((tm, D_MODEL), row)]
    out_shape = [jax.ShapeDtypeStruct((t, D_MODEL), F32), jax.ShapeDtypeStruct((t, D_MODEL), F32 if moe else BF16)]
    if moe:
        wr_hi, wr_lo = _split_bf16(w_router.T)
        weights.extend([jnp.concatenate([wr_hi, wr_lo]), jnp.concatenate([wr_hi, jnp.zeros_like(wr_hi)])])
        out_specs.append(pl.BlockSpec((tm, LANES), row))
        out_shape.append(jax.ShapeDtypeStruct((t, LANES), F32))
    ng = len(A_GROUPS)
    return pl.pallas_call(
        functools.partial(_merge_kernel, moe=moe),
        grid=(t // tm,),
        in_specs=[pl.BlockSpec((tm, D_MODEL), row)]
                 + [_residue_spec(seq, dil, A_WIDTH) for _, dil in A_GROUPS]
                 + [_residue_spec(seq, dil, LANES) for _, dil in A_GROUPS]
                 + [pl.BlockSpec((tm, B_Q_WIDTH), row)] + [full(a) for a in weights],
        out_specs=out_specs,
        out_shape=out_shape,
        scratch_shapes=[pltpu.VMEM((ng, A_WIDTH // LANES, tm, LANES), F32), pltpu.VMEM((ng, tm, LANES), F32),
                        pltpu.VMEM((A_WIDTH // LANES + 1, ROW_STRIDE, tm // ROW_STRIDE, LANES), F32)],
        compiler_params=_params("parallel"),
        name="merge",
    )(x, *outs, *lses, y_b, *weights)


def _ffn_kernel(h_ref, x_ref, wg_ref, wu_ref, wd_ref, o_ref, hm_ref):
    h = h_ref[...]
    d_ff = wg_ref.shape[1]
    for lo in range(0, d_ff, FF_CHUNK_DENSE):
        cols = slice(lo, min(lo + FF_CHUNK_DENSE, d_ff))
        hm_ref[:, cols] = (jax.nn.silu(_dot(h, wg_ref[:, cols])) * _dot(h, wu_ref[:, cols])).astype(BF16)
    o_ref[...] = x_ref[...] + _dot(hm_ref[...], wd_ref[...])


def _ffn(h2, x, w_g, w_u, w_d):
    t = x.shape[0]
    tm = TOKEN_TILE
    d_ff = w_g.shape[1]
    row = lambda i: (i, 0)
    fixed = lambda i: (0, 0)
    return pl.pallas_call(
        _ffn_kernel,
        grid=(t // tm,),
        in_specs=[pl.BlockSpec((tm, D_MODEL), row), pl.BlockSpec((tm, D_MODEL), row),
                  pl.BlockSpec((D_MODEL, d_ff), fixed), pl.BlockSpec((D_MODEL, d_ff), fixed),
                  pl.BlockSpec((d_ff, D_MODEL), fixed)],
        out_specs=pl.BlockSpec((tm, D_MODEL), row),
        out_shape=jax.ShapeDtypeStruct((t, D_MODEL), F32),
        scratch_shapes=[pltpu.VMEM((tm, d_ff), BF16)],
        compiler_params=_params("parallel"),
        name="ffn_dense",
    )(h2, x, w_g, w_u, w_d)


def _dispatch_kernel(zero_ref, dest_ref, h_hbm, xs_ref, hbuf_ref, zbuf_ref, in_sem, out_sem, zsem):
    tm = hbuf_ref.shape[1]
    i = pl.program_id(0)
    steps = pl.num_programs(0)
    load = lambda step, slot: pltpu.make_async_copy(
        h_hbm.at[pl.ds(pl.multiple_of(step * tm, tm), tm), :], hbuf_ref.at[slot], in_sem.at[slot])

    def wait_rows(slot):
        for _ in range(TOP_K):
            pltpu.make_async_copy(hbuf_ref.at[slot], xs_ref.at[pl.ds(0, tm), :], out_sem.at[slot]).wait()

    @pl.when(i == 0)
    def _():
        load(0, 0).start()
        zbuf_ref[...] = jnp.zeros_like(zbuf_ref)
        n_zero = zero_ref.shape[0]
        zero_copy = lambda z: pltpu.make_async_copy(
            zbuf_ref, xs_ref.at[pl.ds(pl.multiple_of(zero_ref[z], MOE_TILE), MOE_TILE), :], zsem)
        for z in range(n_zero):
            zero_copy(z).start()
            zero_copy(z).wait()

    for slot in range(2):
        @pl.when(i % 2 == slot)
        def _():
            load(i, slot).wait()

            @pl.when(i >= 1)
            def _():
                wait_rows(1 - slot)

            @pl.when(i + 1 < steps)
            def _():
                load(i + 1, 1 - slot).start()

            for r in range(tm):
                for k in range(TOP_K):
                    d = dest_ref[0, 0, TOP_K * r + k]
                    pltpu.make_async_copy(hbuf_ref.at[slot, pl.ds(r, 1), :], xs_ref.at[pl.ds(d, 1), :],
                                          out_sem.at[slot]).start(priority=k)

            @pl.when(i == steps - 1)
            def _():
                wait_rows(slot)


def _dispatch(h2, dest, zero_blocks, cap):
    t = h2.shape[0]
    tm = DISPATCH_TILE
    return pl.pallas_call(
        _dispatch_kernel,
        grid_spec=pltpu.PrefetchScalarGridSpec(
            num_scalar_prefetch=1,
            grid=(t // tm,),
            in_specs=[pl.BlockSpec((1, 1, TOP_K * tm), lambda i, z: (i, 0, 0), memory_space=pltpu.SMEM),
                      pl.BlockSpec(memory_space=pl.ANY)],
            out_specs=pl.BlockSpec(memory_space=pl.ANY),
            scratch_shapes=[pltpu.VMEM((2, tm, D_MODEL), F32), pltpu.VMEM((MOE_TILE, D_MODEL), F32),
                            pltpu.SemaphoreType.DMA((2,)), pltpu.SemaphoreType.DMA((2,)),
                            pltpu.SemaphoreType.DMA(())],
        ),
        out_shape=jax.ShapeDtypeStruct((cap, D_MODEL), F32),
        compiler_params=_params("arbitrary"),
        name="dispatch",
    )(zero_blocks, dest.reshape(t // tm, 1, TOP_K * tm), h2)


def _expert_kernel(be_ref, used_ref, xs_ref, wg_ref, wu_ref, wd_ref, ys_ref, xb_ref, hm_ref):
    del be_ref
    i, j = pl.program_id(0), pl.program_id(1)
    last = pl.num_programs(1) - 1
    tf = wg_ref.shape[2]
    active = i < used_ref[0]

    @pl.when(active & (j == 0))
    def _():
        xb_ref[...] = xs_ref[...].astype(BF16)

    for c in range(FF_SPLIT_EXPERT):
        @pl.when(active & (j == c))
        def _():
            xb = xb_ref[...]
            hm_ref[:, c * tf:(c + 1) * tf] = (jax.nn.silu(_dot(xb, wg_ref[0])) * _dot(xb, wu_ref[0])).astype(BF16)

    @pl.when(active & (j == last))
    def _():
        ys_ref[...] = _dot(hm_ref[...], wd_ref[0])

    @pl.when(jnp.logical_not(active) & (j == last))
    def _():
        ys_ref[...] = jnp.zeros_like(ys_ref)


def _experts(xs, block_e, n_used, w_g, w_u, w_d):
    cap = xs.shape[0]
    tm = MOE_TILE
    d_ff = w_g.shape[2]
    tf = d_ff // FF_SPLIT_EXPERT
    assert tf * FF_SPLIT_EXPERT == d_ff and tf % LANES == 0
    chunk = lambda i, j, used: jnp.where(i < used[0], j, FF_SPLIT_EXPERT - 1)
    return pl.pallas_call(
        _expert_kernel,
        grid_spec=pltpu.PrefetchScalarGridSpec(
            num_scalar_prefetch=2,
            grid=(cap // tm, FF_SPLIT_EXPERT),
            in_specs=[pl.BlockSpec((tm, D_MODEL), lambda i, j, be, used: (i, 0)),
                      pl.BlockSpec((1, D_MODEL, tf), lambda i, j, be, used: (be[i], 0, chunk(i, j, used))),
                      pl.BlockSpec((1, D_MODEL, tf), lambda i, j, be, used: (be[i], 0, chunk(i, j, used))),
                      pl.BlockSpec((1, d_ff, D_MODEL), lambda i, j, be, used: (be[i], 0, 0))],
            out_specs=pl.BlockSpec((tm, D_MODEL), lambda i, j, be, used: (i, 0)),
            scratch_shapes=[pltpu.VMEM((tm, D_MODEL), BF16), pltpu.VMEM((tm, d_ff), BF16)],
        ),
        out_shape=jax.ShapeDtypeStruct((cap, D_MODEL), F32),
        compiler_params=_params("arbitrary", "arbitrary"),
        name="experts",
    )(block_e, n_used, xs, w_g, w_u, w_d)


def _combine_kernel(dest_ref, dest_next_ref, r_ref, x_ref, ys_ref, g_ref, o_ref, y_ref, sem):
    tm = x_ref.shape[0]
    i = pl.program_id(0)

    def fetch(idx_ref, slot):
        for r in range(tm):
            for k in range(TOP_K):
                d = idx_ref[0, 0, TOP_K * r + k]
                pltpu.make_async_copy(ys_ref.at[pl.ds(d, 1), :], y_ref.at[slot, k, pl.ds(r, 1), :],
                                      sem.at[slot, k]).start(priority=k)

    @pl.when(i == 0)
    def _():
        fetch(dest_ref, 0)

    for parity in range(2):
        @pl.when((i + 1 < pl.num_programs(0)) & ((i + 1) % 2 == parity))
        def _():
            fetch(dest_next_ref, parity)

    slot = i % 2
    for k in range(TOP_K):
        pltpu.make_async_copy(ys_ref.at[pl.ds(0, tm), :], y_ref.at[slot, k], sem.at[slot, k]).wait()
    gates = r_ref[...]
    moe = gates[:, 2:3] * y_ref[slot, 0] + gates[:, 3:4] * y_ref[slot, 1]
    o_ref[...] = _rms(x_ref[...] + moe, g_ref[...])


def _combine(ys, dest, rout, x, g_final):
    t = x.shape[0]
    tm = COMBINE_TILE
    steps = t // tm
    dest3 = dest.reshape(steps, 1, TOP_K * tm)
    return pl.pallas_call(
        _combine_kernel,
        grid=(steps,),
        in_specs=[pl.BlockSpec((1, 1, TOP_K * tm), lambda i: (i, 0, 0), memory_space=pltpu.SMEM),
                  pl.BlockSpec((1, 1, TOP_K * tm), lambda i: (jnp.minimum(i + 1, steps - 1), 0, 0),
                               memory_space=pltpu.SMEM),
                  pl.BlockSpec((tm, LANES), lambda i: (i, 0)),
                  pl.BlockSpec((tm, D_MODEL), lambda i: (i, 0)),
                  pl.BlockSpec(memory_space=pl.ANY),
                  pl.BlockSpec((1, D_MODEL), lambda i: (0, 0))],
        out_specs=pl.BlockSpec((tm, D_MODEL), lambda i: (i, 0)),
        out_shape=jax.ShapeDtypeStruct((t, D_MODEL), F32),
        scratch_shapes=[pltpu.VMEM((2, TOP_K, tm, D_MODEL), F32), pltpu.SemaphoreType.DMA((2, TOP_K))],
        compiler_params=_params("arbitrary"),
        name="combine",
    )(dest3, dest3, rout, x, ys, g_final)


def _routing_tables(rout, t):
    tm = MOE_TILE
    e_flat = rout[:, :TOP_K].astype(jnp.int32).reshape(-1)
    onehot = (e_flat[:, None] == jnp.arange(N_EXPERTS, dtype=jnp.int32)[None, :]).astype(jnp.int32)
    csum = jnp.cumsum(onehot, axis=0)
    counts = csum[-1]
    padded = (counts + tm - 1) // tm * tm
    pend = jnp.cumsum(padded)
    pstart = pend - padded
    dest = jnp.sum(onehot * (csum - 1 + pstart[None, :]), axis=1).astype(jnp.int32)
    n_blocks = -(-(t * TOP_K) // tm) + N_EXPERTS
    block_start = jnp.arange(n_blocks, dtype=jnp.int32) * tm
    block_e = jnp.minimum(jnp.sum((pend[None, :] <= block_start[:, None]).astype(jnp.int32), axis=1), N_EXPERTS - 1)
    zero_blocks = jnp.concatenate([jnp.maximum(pend - tm, 0), block_start[n_blocks - N_EXPERTS:]]).astype(jnp.int32)
    n_used = (pend[N_EXPERTS - 1:] // tm).astype(jnp.int32)
    return dest, block_e, n_used, zero_blocks, n_blocks * tm


def _trunk(x3, wts):
    bsz, seq, _ = x3.shape
    assert seq % TOKEN_TILE == 0
    x = x3.reshape(bsz * seq, D_MODEL)
    depth = len(wts["layers"])
    for li, lw in enumerate(wts["layers"]):
        *pas, qb, kvb = _proj(x, lw["norm_mix_g"], lw["w_in"], bsz, seq)
        outs, lses = zip(*[_attn_a(pa, gi) for gi, pa in enumerate(pas)])
        y_b = _attn_b(qb, kvb, lw["sink"], bsz, seq)
        mixer = (lw["norm_mix_g"], lw["w_gate"], lw["b_gate"], lw["w_proj_a"], lw["w_proj_b"], lw["w_out"],
                 lw["norm_ffn_g"])
        if "w_router" not in lw:
            assert li < depth - 1
            x, h2 = _merge(x, outs, lses, y_b, seq, *mixer)
            x = _ffn(h2, x, lw["w_ff_gate"], lw["w_ff_up"], lw["w_ff_down"])
        else:
            assert li == depth - 1
            x, h2, rout = _merge(x, outs, lses, y_b, seq, *mixer, w_router=lw["w_router"])
            dest, block_e, n_used, zero_blocks, cap = _routing_tables(rout, bsz * seq)
            xs = _dispatch(h2, dest, zero_blocks, cap)
            ys = _experts(xs, block_e, n_used, lw["w_e_gate"], lw["w_e_up"], lw["w_e_down"])
            x = _combine(ys, dest, rout, x, wts["norm_final_g"])
    return x.reshape(bsz, seq, D_MODEL)


def kernel(x_prompt, x_sample, norm_mix_g, w_in, w_gate, b_gate, w_proj_a, w_proj_b, w_out, sink, norm_ffn_g,
           w_ff_gate, w_ff_up, w_ff_down, w_router, w_e_gate, w_e_up, w_e_down, norm_final_g):
    depth = w_in.shape[0]
    layers = []
    for li in range(depth):
        lw = {
            "norm_mix_g": norm_mix_g[li].reshape(1, D_MODEL), "w_in": w_in[li].astype(BF16),
            "w_gate": w_gate[li].astype(BF16), "b_gate": b_gate[li].reshape(1, 2 * D_MODEL),
            "w_proj_a": w_proj_a[li].astype(BF16), "w_proj_b": w_proj_b[li].astype(BF16),
            "w_out": w_out[li].astype(BF16), "sink": sink[li], "norm_ffn_g": norm_ffn_g[li].reshape(1, D_MODEL),
        }
        i = li // 2
        if li % 2 == 0:
            lw.update(w_ff_gate=w_ff_gate[i].astype(BF16), w_ff_up=w_ff_up[i].astype(BF16),
                      w_ff_down=w_ff_down[i].astype(BF16))
        else:
            lw.update(w_router=w_router[i], w_e_gate=w_e_gate[i].astype(BF16), w_e_up=w_e_up[i].astype(BF16),
                      w_e_down=w_e_down[i].astype(BF16))
        layers.append(lw)
    wts = {"layers": layers, "norm_final_g": norm_final_g.reshape(1, D_MODEL)}
    return _trunk(x_prompt, wts), _trunk(x_sample, wts)
```
